```python
import math
import jax, jax.numpy as jnp
from jax import lax
import numpy as np

D_MODEL = 1024
BATCH = 8
SEQ = 2048
DEPTH = 2
DEC_BATCH = 128
DEC_SEQ = 8
PAST_LEN = 2048
PAGE_SIZE = 128

N_A_LAYERS = DEPTH // 2
N_B_LAYERS = DEPTH - N_A_LAYERS
CHUNK = 128
D_A = D_MODEL
A_GROUPS = 8
A_GROUP_DIM = D_A // A_GROUPS
N_HEADS = 16
HEAD_DIM = D_MODEL // N_HEADS
MOBA_BLOCK = 256
MOBA_TOPK = 3
Q_BLOCK = 128
N_BUCKETS = 32
MAX_DISTANCE = 128
PEER_HEADS = 8
PEER_NKEYS = 128
PEER_EXPERTS = PEER_NKEYS * PEER_NKEYS
PEER_DKEY = 256
PEER_TOPK = 16
PEER_BLOCK = 128
EPS = 1e-6
NEG = -1e30

kernel_name = 'yoco_gmlp_moba_peer_step'


def _rmsnorm(x, g):
    xf = x.astype(jnp.float32)
    y = xf * lax.rsqrt(jnp.mean(xf * xf, axis=-1, keepdims=True) + EPS)
    return y.astype(x.dtype) * g


def _t5_bucket(rel):
    n = jnp.maximum(rel, 0)
    max_exact = N_BUCKETS // 2
    nf = jnp.maximum(n, max_exact).astype(jnp.float32)
    large = max_exact + (jnp.log(nf / max_exact) / math.log(MAX_DISTANCE / max_exact)
                         * (N_BUCKETS - max_exact)).astype(jnp.int32)
    large = jnp.minimum(large, N_BUCKETS - 1)
    return jnp.where(n < max_exact, n, large)


def _chunk_mlp(h, w_in, g_v, w_s, b_s, w_out):
    B, T, _ = h.shape
    u, v = jnp.split(jax.nn.gelu(h @ w_in), 2, axis=-1)
    v = _rmsnorm(v, g_v)
    L = min(T, CHUNK)
    n_c = T // L
    w = jnp.where(jnp.tril(jnp.ones((L, L), bool)), w_s[:, :L, :L], 0)
    vc = v.reshape(B, n_c, L, A_GROUPS, A_GROUP_DIM)
    s = jnp.einsum('gts,bnsgc->bntgc', w, vc) + b_s[:, :L].T[None, None, :, :, None]
    out = (u * s.reshape(B, T, D_A)) @ w_out
    return out, v


def _peer(h, w_q, sub_keys, w_u, w_v):
    B, T, D = h.shape
    n = B * T
    nblk = -(-n // PEER_BLOCK)
    xb = jnp.pad(h.reshape(n, D), ((0, nblk * PEER_BLOCK - n), (0, 0))).reshape(nblk, PEER_BLOCK, D)

    def step(xc):
        q = (xc @ w_q).reshape(PEER_BLOCK, PEER_HEADS, 2, PEER_DKEY // 2).astype(jnp.float32)
        s = jnp.einsum('nhpd,phkd->nhpk', q, sub_keys.astype(jnp.float32))
        sv, si = lax.top_k(s, PEER_TOPK)
        cand = (sv[:, :, 0, :, None] + sv[:, :, 1, None, :]).reshape(PEER_BLOCK, PEER_HEADS, -1)
        cidx = (si[:, :, 0, :, None] * PEER_NKEYS + si[:, :, 1, None, :]).reshape(PEER_BLOCK, PEER_HEADS, -1)
        fs, fpos = lax.top_k(cand, PEER_TOPK)
        eidx = jnp.take_along_axis(cidx, fpos, axis=-1)
        g = jax.nn.softmax(fs, axis=-1)
        u = w_u[eidx]
        a = jax.nn.gelu(jnp.einsum('nhkd,nd->nhk', u, xc).astype(jnp.float32))
        vv = w_v[eidx]
        return jnp.einsum('nhk,nhkd->nd', (g * a).astype(xc.dtype), vv).astype(xc.dtype)

    y = lax.map(step, xb).reshape(nblk * PEER_BLOCK, D)[:n]
    return y.reshape(B, T, D)


def _moba_core(q, qpos, k_sel, v_sel, kpos_sel, sel_valid, k_own, v_own, kpos_own, rel_bias):
    Tq, H, Dh = q.shape
    scale = HEAD_DIM ** -0.5
    qf = q.astype(jnp.float32)
    rb = rel_bias.astype(jnp.float32)
    rel_own = qpos[:, None] - kpos_own[None, :]
    lo = jnp.einsum('thd,lhd->thl', qf, k_own.astype(jnp.float32)) * scale
    lo = lo + jnp.transpose(rb[_t5_bucket(rel_own)], (0, 2, 1))
    lo = jnp.where((rel_own >= 0)[:, None, :], lo, NEG)
    if k_sel is None:
        p = jax.nn.softmax(lo, axis=-1)
        out = jnp.einsum('thl,lhd->thd', p, v_own.astype(jnp.float32))
        return out.astype(q.dtype)
    h_idx = jnp.arange(H)[None, :, None, None]
    ls = jnp.einsum('thd,thkld->thkl', qf, k_sel.astype(jnp.float32)) * scale
    ls = ls + rb[_t5_bucket(qpos[:, None, None, None] - kpos_sel), h_idx]
    if sel_valid is not None:
        ls = jnp.where(sel_valid[..., None], ls, NEG)
    ns = ls.shape[2] * ls.shape[3]
    p = jax.nn.softmax(jnp.concatenate([ls.reshape(Tq, H, ns), lo], axis=-1), axis=-1)
    out = (jnp.einsum('thn,thnd->thd', p[..., :ns], v_sel.reshape(Tq, H, ns, Dh).astype(jnp.float32))
           + jnp.einsum('thl,lhd->thd', p[..., ns:], v_own.astype(jnp.float32)))
    return out.astype(q.dtype)


def _moba_prompt(q, k, v, rel_bias):
    B, S, H, Dh = q.shape
    nb = -(-S // MOBA_BLOCK)
    pad = ((0, 0), (0, nb * MOBA_BLOCK - S), (0, 0), (0, 0))
    kb = jnp.pad(k, pad).reshape(B, nb, MOBA_BLOCK, H, Dh)
    vb = jnp.pad(v, pad).reshape(B, nb, MOBA_BLOCK, H, Dh)
    means = jnp.mean(kb.astype(jnp.float32), axis=2)
    nsel = min(MOBA_TOPK, nb)
    nqb = S // Q_BLOCK
    qb = q.reshape(B * nqb, Q_BLOCK, H, Dh)
    ids = jnp.arange(B * nqb)
    blk = jnp.arange(MOBA_BLOCK)
    h_idx = jnp.arange(H)[None, :, None]

    def step(args):
        q_blk, i = args
        bi = i // nqb
        start = (i % nqb) * Q_BLOCK
        qpos = start + jnp.arange(Q_BLOCK)
        b_own = start // MOBA_BLOCK
        kb_b = kb[bi]
        vb_b = vb[bi]
        gate = jnp.einsum('thd,jhd->thj', q_blk.astype(jnp.float32), means[bi])
        gate = jnp.where(jnp.arange(nb) < b_own, gate, -jnp.inf)
        _, sel = lax.top_k(gate, nsel)
        valid = sel < b_own
        k_sel = kb_b[sel, :, h_idx]
        v_sel = vb_b[sel, :, h_idx]
        kpos_sel = sel[..., None] * MOBA_BLOCK + blk
        k_own = lax.dynamic_index_in_dim(kb_b, b_own, axis=0, keepdims=False)
        v_own = lax.dynamic_index_in_dim(vb_b, b_own, axis=0, keepdims=False)
        kpos_own = b_own * MOBA_BLOCK + blk
        return _moba_core(q_blk, qpos, k_sel, v_sel, kpos_sel, valid, k_own, v_own, kpos_own, rel_bias)

    out = lax.map(step, (qb, ids))
    return out.reshape(B, S, H, Dh)


def _moba_sample(q, k_new, v_new, cache_k, cache_v, page_table, rel_bias):
    DB, DS, H, Dh = q.shape
    n_pages = page_table.shape[1]
    past = n_pages * PAGE_SIZE
    ppb = MOBA_BLOCK // PAGE_SIZE
    b_own = past // MOBA_BLOCK
    nsel = min(MOBA_TOPK, b_own)
    own_page0 = b_own * ppb
    own_rows = (n_pages - own_page0) * PAGE_SIZE
    qpos = past + jnp.arange(DS)
    kpos_own = jnp.arange(b_own * MOBA_BLOCK, past + DS)
    blk = jnp.arange(MOBA_BLOCK)
    h_idx = jnp.arange(H)

    def one_seq(args):
        q_s, kn, vn, pt = args
        own_pages = pt[own_page0:]
        k_own = jnp.concatenate([cache_k[own_pages].reshape(own_rows, H, Dh), kn], axis=0)
        v_own = jnp.concatenate([cache_v[own_pages].reshape(own_rows, H, Dh), vn], axis=0)
        if nsel == 0:
            return _moba_core(q_s, qpos, None, None, None, None, k_own, v_own, kpos_own, rel_bias)
        k_past = cache_k[pt[:own_page0]].reshape(b_own, MOBA_BLOCK, H, Dh)
        means = jnp.mean(k_past.astype(jnp.float32), axis=1)
        gate = jnp.einsum('thd,jhd->thj', q_s.astype(jnp.float32), means)
        _, sel = lax.top_k(gate, nsel)
        k_sel = k_past[sel, :, h_idx[None, :, None]]
        pages = pt[sel[..., None] * ppb + jnp.arange(ppb)]
        v_sel = cache_v[pages, :, h_idx[None, :, None, None]].reshape(DS, H, nsel, MOBA_BLOCK, Dh)
        kpos_sel = sel[..., None] * MOBA_BLOCK + blk
        return _moba_core(q_s, qpos, k_sel, v_sel, kpos_sel, None, k_own, v_own, kpos_own, rel_bias)

    return lax.map(one_seq, (q, k_new, v_new, page_table))


def _trunk(x, c, attend, p):
    B, T, _ = x.shape
    k = None
    v = None
    chunk_v = []
    for l in range(DEPTH):
        mod = jax.nn.silu(c) @ p['w_ada'][l] + p['b_ada'][l]
        sh1, sc1, g1, sh2, sc2, g2 = jnp.split(mod[:, None, :], 6, axis=-1)
        h = _rmsnorm(x, p['g_mix'][l]) * (1 + sc1) + sh1
        if l < N_A_LAYERS:
            out, v_rows = _chunk_mlp(h, p['a_w_in'][l], p['a_g_v'][l], p['a_w_s'][l],
                                     p['a_b_s'][l], p['a_w_out'][l])
            chunk_v.append(v_rows)
        else:
            j = l - N_A_LAYERS
            q = (h @ p['b_w_q'][j]).reshape(B, T, N_HEADS, HEAD_DIM)
            out = attend(q, k, v).reshape(B, T, N_HEADS * HEAD_DIM) @ p['b_w_o'][j]
        x = x + g1 * out
        h = _rmsnorm(x, p['g_ffn'][l]) * (1 + sc2) + sh2
        x = x + g2 * _peer(h, p['p_w_q'][l], p['p_sub_keys'][l], p['p_w_u'][l], p['p_w_v'][l])
        if l == N_A_LAYERS - 1:
            kv = (_rmsnorm(x, p['kv_g']) @ p['w_kv']).reshape(B, T, 2, N_HEADS, HEAD_DIM)
            k = kv[:, :, 0]
            v = kv[:, :, 1]
    return _rmsnorm(x, p['final_g']), k, v, jnp.stack(chunk_v)


def setup_inputs(seed: int = 0) -> dict:
    key = jax.random.key(seed)
    ks = iter(jax.random.split(key, 40))

    def nrm(shape, s):
        return jax.random.normal(next(ks), shape, jnp.float32) * s

    def gain(shape):
        return 1.0 + nrm(shape, 0.01)

    n_pages = PAST_LEN // PAGE_SIZE
    n_pool = (DEC_BATCH * n_pages * 5 + 3) // 4
    perm = jax.random.permutation(next(ks), n_pool)
    page_table = perm[:DEC_BATCH * n_pages].reshape(DEC_BATCH, n_pages).astype(jnp.int32)
    D = D_MODEL
    return {
        'x_prompt': nrm((BATCH, SEQ, D), 1.0),
        'x_sample': nrm((DEC_BATCH, DEC_SEQ, D), 1.0),
        'cache_k': nrm((n_pool, PAGE_SIZE, N_HEADS, HEAD_DIM), 1.0),
        'cache_v': nrm((n_pool, PAGE_SIZE, N_HEADS, HEAD_DIM), 1.0),
        'page_table': page_table,
        'c_prompt': nrm((BATCH, D), 1.0),
        'c_sample': nrm((DEC_BATCH, D), 1.0),
        'a_w_in': nrm((N_A_LAYERS, D, 2 * D_A), D ** -0.5),
        'a_g_v': gain((N_A_LAYERS, D_A)),
        'a_w_s': nrm((N_A_LAYERS, A_GROUPS, CHUNK, CHUNK), CHUNK ** -0.5),
        'a_b_s': gain((N_A_LAYERS, A_GROUPS, CHUNK)),
        'a_w_out': nrm((N_A_LAYERS, D_A, D), D_A ** -0.5),
        'b_w_q': nrm((N_B_LAYERS, D, N_HEADS * HEAD_DIM), D ** -0.5),
        'b_w_o': nrm((N_B_LAYERS, N_HEADS * HEAD_DIM, D), (N_HEADS * HEAD_DIM) ** -0.5),
        'kv_g': gain((D,)),
        'w_kv': nrm((D, 2 * N_HEADS * HEAD_DIM), D ** -0.5),
        'rel_bias': nrm((N_BUCKETS, N_HEADS), 0.5),
        'g_mix': gain((DEPTH, D)),
        'g_ffn': gain((DEPTH, D)),
        'w_ada': nrm((DEPTH, D, 6 * D), 0.5 * D ** -0.5),
        'b_ada': nrm((DEPTH, 6 * D), 0.02),
        'p_w_q': nrm((DEPTH, D, PEER_HEADS * PEER_DKEY), D ** -0.5),
        'p_sub_keys': nrm((DEPTH, 2, PEER_HEADS, PEER_NKEYS, PEER_DKEY // 2), (PEER_DKEY // 2) ** -0.5),
        'p_w_u': nrm((DEPTH, PEER_EXPERTS, D), D ** -0.5),
        'p_w_v': nrm((DEPTH, PEER_EXPERTS, D), 0.5),
        'final_g': gain((D,)),
    }


def reference(x_prompt, x_sample, cache_k, cache_v, page_table, c_prompt, c_sample,
              a_w_in, a_g_v, a_w_s, a_b_s, a_w_out, b_w_q, b_w_o, kv_g, w_kv, rel_bias,
              g_mix, g_ffn, w_ada, b_ada, p_w_q, p_sub_keys, p_w_u, p_w_v, final_g):
    p = dict(a_w_in=a_w_in, a_g_v=a_g_v, a_w_s=a_w_s, a_b_s=a_b_s, a_w_out=a_w_out,
             b_w_q=b_w_q, b_w_o=b_w_o, kv_g=kv_g, w_kv=w_kv, g_mix=g_mix, g_ffn=g_ffn,
             w_ada=w_ada, b_ada=b_ada, p_w_q=p_w_q, p_sub_keys=p_sub_keys, p_w_u=p_w_u,
             p_w_v=p_w_v, final_g=final_g)

    def attend_prompt(q, k, v):
        return _moba_prompt(q, k, v, rel_bias)

    def attend_sample(q, k, v):
        return _moba_sample(q, k, v, cache_k, cache_v, page_table, rel_bias)

    y_prompt, k_prompt, v_prompt, _ = _trunk(x_prompt, c_prompt, attend_prompt, p)
    y_sample, k_sample, v_sample, chunk_v_sample = _trunk(x_sample, c_sample, attend_sample, p)
    return (y_prompt, y_sample, k_prompt, v_prompt, k_sample, v_sample, chunk_v_sample)
```

```python
import functools
import math

import jax
import jax.numpy as jnp
from jax import lax
from jax.experimental import pallas as pl
from jax.experimental.pallas import tpu as pltpu

D_MODEL = 1024
N_HEADS = 16
HEAD_DIM = 64
CHUNK = 128
A_GROUPS = 8
MOBA_BLOCK = 256
MOBA_TOPK = 3
PAGE_SIZE = 128
N_BUCKETS = 32
MAX_DISTANCE = 128
PEER_HEADS = 8
PEER_NKEYS = 128
PEER_TOPK = 16
EPS = 1e-6
NEG = -1e30
NINF = float("-inf")
SCALE = HEAD_DIM ** -0.5

BF = jnp.bfloat16
F32 = jnp.float32

LANES = 128
VMEM_LIMIT = 56 * 1024 * 1024

_NT = (((1,), (1,)), ((), ()))


def _cparams(*sem):
    return pltpu.CompilerParams(dimension_semantics=sem, vmem_limit_bytes=VMEM_LIMIT)


def _rms(x, g):
    ms = jnp.mean(x * x, axis=-1, keepdims=True)
    return x * lax.rsqrt(ms + EPS) * g


def _dot(a, b):
    return jnp.dot(a, b, preferred_element_type=F32)


def _dot_nt(a, b):
    return lax.dot_general(a, b, _NT, preferred_element_type=F32)


def _ada_kernel(c_ref, w_ref, b_ref, o_ref):
    c = c_ref[...]
    s = (c * jax.nn.sigmoid(c)).astype(BF)
    o_ref[0] = _dot(s, w_ref[0].astype(BF)) + b_ref[0]


def _ada(c_all, w_ada, b_ada):
    m = c_all.shape[0]
    depth, _, n6 = w_ada.shape
    tn = 1536
    return pl.pallas_call(
        _ada_kernel,
        grid=(depth, n6 // tn),
        in_specs=[
            pl.BlockSpec((m, D_MODEL), lambda l, j: (0, 0)),
            pl.BlockSpec((1, D_MODEL, tn), lambda l, j: (l, 0, j)),
            pl.BlockSpec((1, 1, tn), lambda l, j: (l, 0, j)),
        ],
        out_specs=pl.BlockSpec((1, m, tn), lambda l, j: (l, 0, j)),
        out_shape=jax.ShapeDtypeStruct((depth, m, n6), F32),
        compiler_params=_cparams("arbitrary", "arbitrary"),
        name="ada",
    )(c_all, w_ada, b_ada.reshape(depth, 1, n6))


def _mod_operand(arr, t_len, tm):
    b = arr.shape[0]
    if t_len % tm == 0:
        per = t_len // tm
        return arr.reshape(b, 1, D_MODEL), pl.BlockSpec((1, 1, D_MODEL), lambda i, *_: (i // per, 0, 0))
    n = b * t_len
    rep = jnp.repeat(arr, t_len, axis=0).reshape(n // tm, tm, D_MODEL)
    return rep, pl.BlockSpec((1, tm, D_MODEL), lambda i, *_: (i, 0, 0))


def _gmlp_kernel(x_ref, gm_ref, sc_ref, sh_ref, g1_ref, win_ref, gv_ref, wg_ref, bs_ref, wout_ref,
                 xo_ref, v_ref, us_scr, *, tm):
    x = x_ref[...]
    h = _rms(x, gm_ref[...]) * (1.0 + sc_ref[0]) + sh_ref[0]
    uv = jax.nn.gelu(_dot(h.astype(BF), win_ref[...]))
    u = uv[:, :D_MODEL]
    v = _rms(uv[:, D_MODEL:], gv_ref[...])
    v_ref[...] = v
    vb = v.astype(BF)
    for c in range(tm // CHUNK):
        r0 = c * CHUNK
        parts = []
        for g in range(A_GROUPS):
            c0 = g * CHUNK
            parts.append(_dot(wg_ref[g], vb[r0:r0 + CHUNK, c0:c0 + CHUNK]))
        s = jnp.concatenate(parts, axis=1) + bs_ref[...]
        us_scr[r0:r0 + CHUNK, :] = (u[r0:r0 + CHUNK, :] * s).astype(BF)
    out = _dot(us_scr[...], wout_ref[...])
    xo_ref[...] = x + g1_ref[0] * out


def _gmlp(x, gm, sc, sh, g1, w_in, g_v, wg, bs, w_out, t_len, tm):
    n = x.shape[0]
    sc_a, sc_s = _mod_operand(sc, t_len, tm)
    sh_a, sh_s = _mod_operand(sh, t_len, tm)
    g1_a, g1_s = _mod_operand(g1, t_len, tm)
    full = lambda shape: pl.BlockSpec(shape, lambda i: (0,) * len(shape))
    return pl.pallas_call(
        functools.partial(_gmlp_kernel, tm=tm),
        grid=(n // tm,),
        in_specs=[
            pl.BlockSpec((tm, D_MODEL), lambda i: (i, 0)),
            full((1, D_MODEL)), sc_s, sh_s, g1_s,
            full((D_MODEL, 2 * D_MODEL)), full((1, D_MODEL)),
            full((A_GROUPS, CHUNK, CHUNK)), full((CHUNK, D_MODEL)),
            full((D_MODEL, D_MODEL)),
        ],
        out_specs=[pl.BlockSpec((tm, D_MODEL), lambda i: (i, 0)),
                   pl.BlockSpec((tm, D_MODEL), lambda i: (i, 0))],
        out_shape=[jax.ShapeDtypeStruct((n, D_MODEL), F32), jax.ShapeDtypeStruct((n, D_MODEL), F32)],
        scratch_shapes=[pltpu.VMEM((tm, D_MODEL), BF)],
        compiler_params=_cparams("arbitrary"),
        name="gmlp",
    )(x, gm, sc_a, sh_a, g1_a, w_in, g_v, wg, bs, w_out)


def _top16_rows(blk, out_scr):
    s = blk
    for k in range(PEER_TOPK):
        m = jnp.max(s, axis=0, keepdims=True)
        out_scr[k:k + 1, :] = m
        if k + 1 < PEER_TOPK:
            s = jnp.where(s == m, NINF, s)


def _kth_sum(a, b):
    row = lax.broadcasted_iota(jnp.int32, (8, a.shape[1]), 0)
    cands = [a[0:8] + b[0:1], a[8:16] + b[0:1]]
    for l in range(1, 8):
        kmax = PEER_TOPK // (l + 1)
        c = a[0:8] + b[l:l + 1]
        if kmax < 8:
            c = jnp.where(row < kmax, c, NINF)
        cands.append(c)
    cands.append(b[8:16] + a[0:1])
    m0 = None
    z = None
    m = None
    for k in range(PEER_TOPK):
        m = functools.reduce(jnp.maximum, cands)
        m = jnp.max(m, axis=0, keepdims=True)
        if k == 0:
            m0 = m
            z = jnp.ones_like(m)
        else:
            z = z + jnp.exp(m - m0)
        if k + 1 < PEER_TOPK:
            cands = [jnp.where(c == m, NINF, c) for c in cands]
    return m, m0, z


def _route_kernel(x_ref, gm_ref, sc_ref, sh_ref, wqt_ref, sk_ref,
                  ht_ref, s0_ref, s1_ref, e0_ref, e1_ref, thr_ref,
                  qt_scr, a_scr, b_scr, *, tn):
    x = x_ref[...]
    h = _rms(x, gm_ref[...]) * (1.0 + sc_ref[0]) + sh_ref[0]
    ht = h.T.astype(BF)
    ht_ref[...] = ht
    qt_scr[...] = _dot(wqt_ref[...], ht).astype(BF)

    def head(hd, carry):
        for p, s_ref in ((0, s0_ref), (1, s1_ref)):
            r0 = pl.multiple_of((hd * 2 + p) * PEER_NKEYS, PEER_NKEYS)
            s_ref[hd] = _dot(sk_ref[p, hd], qt_scr[pl.ds(r0, PEER_NKEYS), :])
        for tb in range(tn // LANES):
            lanes = slice(tb * LANES, (tb + 1) * LANES)
            s0 = s0_ref[hd, :, lanes]
            s1 = s1_ref[hd, :, lanes]
            _top16_rows(s0, a_scr)
            _top16_rows(s1, b_scr)
            a = a_scr[...]
            b = b_scr[...]
            thr, _, z = _kth_sum(a, b)
            thr_ref[hd, :, lanes] = thr
            e0_ref[hd, :, lanes] = jnp.exp(s0 - a[0:1]) * (1.0 / z)
            e1_ref[hd, :, lanes] = jnp.exp(s1 - b[0:1])
        return carry

    lax.fori_loop(0, PEER_HEADS, head, 0)


def _route(x, gm, sc, sh, wqt, sk, t_len, tn):
    n = x.shape[0]
    sc_a, sc_s = _mod_operand(sc, t_len, tn)
    sh_a, sh_s = _mod_operand(sh, t_len, tn)
    full = lambda shape: pl.BlockSpec(shape, lambda i: (0,) * len(shape))
    sshape = jax.ShapeDtypeStruct((PEER_HEADS, PEER_NKEYS, n), F32)
    sspec = pl.BlockSpec((PEER_HEADS, PEER_NKEYS, tn), lambda i: (0, 0, i))
    return pl.pallas_call(
        functools.partial(_route_kernel, tn=tn),
        grid=(n // tn,),
        in_specs=[
            pl.BlockSpec((tn, D_MODEL), lambda i: (i, 0)),
            full((1, D_MODEL)), sc_s, sh_s,
            full((2 * PEER_HEADS * PEER_NKEYS, D_MODEL)),
            full((2, PEER_HEADS, PEER_NKEYS, PEER_NKEYS)),
        ],
        out_specs=[pl.BlockSpec((D_MODEL, tn), lambda i: (0, i)), sspec, sspec, sspec, sspec,
                   pl.BlockSpec((PEER_HEADS, 1, tn), lambda i: (0, 0, i))],
        out_shape=[jax.ShapeDtypeStruct((D_MODEL, n), BF), sshape, sshape, sshape, sshape,
                   jax.ShapeDtypeStruct((PEER_HEADS, 1, n), F32)],
        scratch_shapes=[pltpu.VMEM((2 * PEER_HEADS * PEER_NKEYS, tn), BF),
                        pltpu.VMEM((PEER_TOPK, LANES), F32), pltpu.VMEM((PEER_TOPK, LANES), F32)],
        compiler_params=_cparams("arbitrary"),
        name="peer_route",
    )(x, gm, sc_a, sh_a, wqt, sk)


PEER_I_PER_STEP = 8


def _expert_kernel(ht_ref, s1_ref, e1_ref, s0_ref, e0_ref, thr_ref, wu_ref, wvt_ref, o_ref,
                   acc_scr, ga_scr, *, tn):
    e = pl.program_id(1)

    @pl.when(e == 0)
    def _():
        acc_scr[...] = jnp.zeros_like(acc_scr)

    a_all = _dot(wu_ref[...], ht_ref[...])
    for il in range(PEER_I_PER_STEP):
        rows = slice(il * PEER_NKEYS, (il + 1) * PEER_NKEYS)
        for tb in range(tn // LANES):
            lanes = slice(tb * LANES, (tb + 1) * LANES)
            g = jnp.zeros((PEER_NKEYS, LANES), F32)
            for hd in range(PEER_HEADS):
                val = s1_ref[hd, :, lanes] + s0_ref[hd, il:il + 1, lanes]
                w = e1_ref[hd, :, lanes] * e0_ref[hd, il:il + 1, lanes]
                g = g + jnp.where(val >= thr_ref[hd, :, lanes], w, 0.0)
            ga_scr[rows, lanes] = (jax.nn.gelu(a_all[rows, lanes]) * g).astype(BF)
    acc_scr[...] += _dot(wvt_ref[...], ga_scr[...])

    @pl.when(e == pl.num_programs(1) - 1)
    def _():
        o_ref[...] = acc_scr[...]


def _experts(ht, s0, s1, e0, e1, thr, wu, wvt, tn):
    n = ht.shape[1]
    n_exp = wu.shape[0]
    te = PEER_I_PER_STEP * PEER_NKEYS
    return pl.pallas_call(
        functools.partial(_expert_kernel, tn=tn),
        grid=(n // tn, n_exp // te),
        in_specs=[
            pl.BlockSpec((D_MODEL, tn), lambda t, e: (0, t)),
            pl.BlockSpec((PEER_HEADS, PEER_NKEYS, tn), lambda t, e: (0, 0, t)),
            pl.BlockSpec((PEER_HEADS, PEER_NKEYS, tn), lambda t, e: (0, 0, t)),
            pl.BlockSpec((PEER_HEADS, PEER_I_PER_STEP, tn), lambda t, e: (0, e, t)),
            pl.BlockSpec((PEER_HEADS, PEER_I_PER_STEP, tn), lambda t, e: (0, e, t)),
            pl.BlockSpec((PEER_HEADS, 1, tn), lambda t, e: (0, 0, t)),
            pl.BlockSpec((te, D_MODEL), lambda t, e: (e, 0)),
            pl.BlockSpec((D_MODEL, te), lambda t, e: (0, e)),
        ],
        out_specs=pl.BlockSpec((D_MODEL, tn), lambda t, e: (0, t)),
        out_shape=jax.ShapeDtypeStruct((D_MODEL, n), F32),
        scratch_shapes=[pltpu.VMEM((D_MODEL, tn), F32), pltpu.VMEM((te, tn), BF)],
        compiler_params=_cparams("arbitrary", "arbitrary"),
        name="peer_experts",
    )(ht, s1, e1, s0, e0, thr, wu, wvt)


def _peer(x, gm, sc, sh, wqt, sk, wu, wvt, t_len):
    n = x.shape[0]
    ht, s0, s1, e0, e1, thr = _route(x, gm, sc, sh, wqt, sk, t_len, 256)
    return _experts(ht, s0, s1, e0, e1, thr, wu, wvt, min(512, n))


def _kvq_kernel(x_ref, pt_ref, g2_ref, kvg_ref, wkv_ref, gm_ref, sc_ref, sh_ref, wq_ref,
                xo_ref, k_ref, v_ref, kb_ref, vb_ref, km_ref, q_ref):
    x = x_ref[...] + g2_ref[0] * pt_ref[...].T
    xo_ref[...] = x
    kv = _dot(_rms(x, kvg_ref[...]).astype(BF), wkv_ref[...])
    k = kv[:, :D_MODEL]
    v = kv[:, D_MODEL:]
    k_ref[...] = k
    v_ref[...] = v
    kb_ref[...] = k.astype(BF)
    vb_ref[...] = v.astype(BF)
    km_ref[0] = jnp.mean(k, axis=0, keepdims=True)
    h = _rms(x, gm_ref[...]) * (1.0 + sc_ref[0]) + sh_ref[0]
    q_ref[...] = _dot(h.astype(BF), wq_ref[...])


def _kvq(x, peer_t, g2, kv_g, w_kv, gm, sc, sh, w_q, t_len):
    n = x.shape[0]
    tm = MOBA_BLOCK
    g2_a, g2_s = _mod_operand(g2, t_len, tm)
    sc_a, sc_s = _mod_operand(sc, t_len, tm)
    sh_a, sh_s = _mod_operand(sh, t_len, tm)
    full = lambda shape: pl.BlockSpec(shape, lambda i: (0,) * len(shape))
    tile = pl.BlockSpec((tm, D_MODEL), lambda i: (i, 0))
    f32 = jax.ShapeDtypeStruct((n, D_MODEL), F32)
    bf = jax.ShapeDtypeStruct((n, D_MODEL), BF)
    return pl.pallas_call(
        _kvq_kernel,
        grid=(n // tm,),
        in_specs=[tile, pl.BlockSpec((D_MODEL, tm), lambda i: (0, i)), g2_s,
                  full((1, D_MODEL)), full((D_MODEL, 2 * D_MODEL)),
                  full((1, D_MODEL)), sc_s, sh_s, full((D_MODEL, D_MODEL))],
        out_specs=[tile, tile, tile, tile, tile,
                   pl.BlockSpec((1, 1, D_MODEL), lambda i: (i, 0, 0)), tile],
        out_shape=[f32, f32, f32, bf, bf, jax.ShapeDtypeStruct((n // tm, 1, D_MODEL), F32), f32],
        compiler_params=_cparams("arbitrary"),
        name="kvq",
    )(x, peer_t, g2_a, kv_g, w_kv, gm, sc_a, sh_a, w_q)


def _third_largest(g):
    m1 = jnp.max(g, axis=1, keepdims=True)
    g2 = jnp.where(g == m1, NINF, g)
    m2 = jnp.max(g2, axis=1, keepdims=True)
    g3 = jnp.where(g2 == m2, NINF, g2)
    return jnp.max(g3, axis=1, keepdims=True)


def _attn_prompt_kernel(c31_ref, q_ref, k_ref, v_ref, km_ref, bo_ref, bp_ref, o_ref, *, nb):
    hp = pl.program_id(1)
    i = pl.program_id(2)
    tq = MOBA_BLOCK
    q2 = q_ref[...]
    lane = lax.broadcasted_iota(jnp.int32, (tq, LANES), 1)
    qf = jnp.concatenate([jnp.where(lane < HEAD_DIM, q2, 0.0),
                          jnp.where(lane >= HEAD_DIM, q2, 0.0)], axis=0)
    qs = qf.astype(BF)

    km = jnp.concatenate([km_ref[0], jnp.zeros((LANES - nb, LANES), F32)], axis=0)
    gate = lax.dot_general(qf, km, _NT, precision=lax.Precision.HIGHEST,
                           preferred_element_type=F32)
    jidx = lax.broadcasted_iota(jnp.int32, (2 * tq, LANES), 1)
    valid = jidx < i
    gm = jnp.where(valid, gate, NINF)
    m3 = _third_largest(gm)
    selpen = jnp.where(jnp.logical_and(gm >= m3, valid), 0.0, NEG)

    row = lax.broadcasted_iota(jnp.int32, (2 * tq, 1), 0)
    c31 = jnp.where(row < tq, c31_ref[2 * hp], c31_ref[2 * hp + 1])

    def update(carry, kblk, vblk, bias):
        m, l, acc = carry
        s = _dot_nt(qs, kblk) * SCALE + bias
        mn = jnp.maximum(m, jnp.max(s, axis=1, keepdims=True))
        alpha = jnp.exp(m - mn)
        p = jnp.exp(s - mn)
        l = alpha * l + jnp.sum(p, axis=1, keepdims=True)
        acc = alpha * acc + _dot(p.astype(BF), vblk)
        return mn, l, acc

    def kv_block(j):
        r0 = pl.multiple_of(j * tq, tq)
        return k_ref[pl.ds(r0, tq), :], v_ref[pl.ds(r0, tq), :]

    def pen(j):
        return jnp.sum(jnp.where(jidx == j, selpen, 0.0), axis=1, keepdims=True)

    kown, vown = kv_block(i)
    s = _dot_nt(qs, kown) * SCALE + bo_ref[0]
    m = jnp.max(s, axis=1, keepdims=True)
    p = jnp.exp(s - m)
    carry = (m, jnp.sum(p, axis=1, keepdims=True), _dot(p.astype(BF), vown))

    def far(j, carry):
        kj, vj = kv_block(j)
        return update(carry, kj, vj, c31 + pen(j))

    carry = lax.fori_loop(0, jnp.maximum(i - 1, 0), far, carry)

    def prev(carry):
        kj, vj = kv_block(i - 1)
        return update(carry, kj, vj, bp_ref[0] + pen(i - 1))

    carry = lax.cond(i >= 1, prev, lambda c: c, carry)
    _, l, acc = carry
    out = acc / l
    o_ref[...] = jnp.where(lane < HEAD_DIM, out[:tq], out[tq:]).astype(o_ref.dtype)


def _attn_prompt(q, kb, vb, kmean, bias_own, bias_prev, c31, batch, seq):
    n = q.shape[0]
    nb = seq // MOBA_BLOCK
    tq = MOBA_BLOCK
    grid_spec = pltpu.PrefetchScalarGridSpec(
        num_scalar_prefetch=1,
        grid=(batch, N_HEADS // 2, nb),
        in_specs=[
            pl.BlockSpec((tq, LANES), lambda b, hp, i, c: (b * nb + i, hp)),
            pl.BlockSpec((seq, LANES), lambda b, hp, i, c: (b, hp)),
            pl.BlockSpec((seq, LANES), lambda b, hp, i, c: (b, hp)),
            pl.BlockSpec((1, nb, LANES), lambda b, hp, i, c: (b, 0, hp)),
            pl.BlockSpec((1, 2 * tq, tq), lambda b, hp, i, c: (hp, 0, 0)),
            pl.BlockSpec((1, 2 * tq, tq), lambda b, hp, i, c: (hp, 0, 0)),
        ],
        out_specs=pl.BlockSpec((tq, LANES), lambda b, hp, i, c: (b * nb + i, hp)),
    )
    return pl.pallas_call(
        functools.partial(_attn_prompt_kernel, nb=nb),
        grid_spec=grid_spec,
        out_shape=jax.ShapeDtypeStruct((n, D_MODEL), BF),
        compiler_params=_cparams("arbitrary", "arbitrary", "arbitrary"),
        name="attn_prompt",
    )(c31, q, kb, vb, kmean.reshape(batch, nb, D_MODEL), bias_own, bias_prev)


def _attn_sample_kernel(pt_ref, q_ref, kn_ref, vn_ref, k0_ref, k1_ref, v0_ref, v1_ref,
                        bias_ref, bown_ref, o_ref, qbd, o_scr, m_scr, l_scr, g_scr, *, nblk, ds):
    j = pl.program_id(1)
    rows = N_HEADS * ds
    row = lax.broadcasted_iota(jnp.int32, (rows, D_MODEL), 0)
    col = lax.broadcasted_iota(jnp.int32, (rows, D_MODEL), 1)
    own = (col // HEAD_DIM) == (row // ds)

    @pl.when(j == 0)
    def _():
        qt = jnp.concatenate([q_ref[0]] * N_HEADS, axis=0)
        qbd[...] = jnp.where(own, qt, 0.0)

    qf = qbd[...]
    qb = qf.astype(BF)
    kb = jnp.concatenate([k0_ref[0], k1_ref[0]], axis=0).astype(BF)
    vb = jnp.concatenate([v0_ref[0], v1_ref[0]], axis=0).astype(BF)
    kmean = jnp.sum(k0_ref[0] + k1_ref[0], axis=0, keepdims=True) * (1.0 / MOBA_BLOCK)
    gate = jnp.sum(qf * kmean, axis=1, keepdims=True)
    s = _dot_nt(qb, kb) * SCALE + bias_ref[0]
    m = jnp.max(s, axis=1, keepdims=True)
    p = jnp.exp(s - m)
    o_scr[j] = _dot(p.astype(BF), vb)
    m_scr[j] = jnp.broadcast_to(m, (rows, LANES))
    l_scr[j] = jnp.broadcast_to(jnp.sum(p, axis=1, keepdims=True), (rows, LANES))
    g_scr[j] = jnp.broadcast_to(gate, (rows, LANES))

    @pl.when(j == nblk - 1)
    def _():
        pad = jnp.zeros((LANES - ds, D_MODEL), F32)
        knb = jnp.concatenate([kn_ref[0], pad], axis=0).astype(BF)
        vnb = jnp.concatenate([vn_ref[0], pad], axis=0).astype(BF)
        so = _dot_nt(qb, knb) * SCALE + bown_ref[...]
        mo = jnp.max(so, axis=1, keepdims=True)
        po = jnp.exp(so - mo)
        lo = jnp.broadcast_to(jnp.sum(po, axis=1, keepdims=True), (rows, LANES))
        mo = jnp.broadcast_to(mo, (rows, LANES))
        oo = _dot(po.astype(BF), vnb)

        gs = [g_scr[b] for b in range(nblk)]
        g1 = functools.reduce(jnp.maximum, gs)
        gs2 = [jnp.where(g == g1, NINF, g) for g in gs]
        g2 = functools.reduce(jnp.maximum, gs2)
        gs3 = [jnp.where(g == g2, NINF, g) for g in gs2]
        g3 = functools.reduce(jnp.maximum, gs3)
        sel = [g >= g3 for g in gs]

        mm = mo
        for b in range(nblk):
            mm = jnp.maximum(mm, jnp.where(sel[b], m_scr[b], NINF))
        wo = jnp.exp(mo - mm)
        lsum = lo * wo
        acc = oo * jnp.concatenate([wo] * (D_MODEL // LANES), axis=1)
        for b in range(nblk):
            wb = jnp.where(sel[b], jnp.exp(m_scr[b] - mm), 0.0)
            lsum = lsum + l_scr[b] * wb
            acc = acc + o_scr[b] * jnp.concatenate([wb] * (D_MODEL // LANES), axis=1)
        inv = 1.0 / lsum
        res = jnp.where(own, acc * jnp.concatenate([inv] * (D_MODEL // LANES), axis=1), 0.0)
        out = res[0:ds]
        for hh in range(1, N_HEADS):
            out = out + res[hh * ds:(hh + 1) * ds]
        o_ref[0] = out


def _attn_sample(q, kn, vn, cache_k, cache_v, page_table, bias_blk, bias_own):
    db, ds, _ = q.shape
    n_pages = page_table.shape[1]
    ppb = MOBA_BLOCK // PAGE_SIZE
    nblk = n_pages // ppb
    rows = N_HEADS * ds
    n_pool = cache_k.shape[0]
    ck = cache_k.reshape(n_pool, PAGE_SIZE, D_MODEL)
    cv = cache_v.reshape(n_pool, PAGE_SIZE, D_MODEL)
    pt = page_table.reshape(-1)

    def page(k):
        return lambda s, j, pt_ref: (pt_ref[s * n_pages + j * ppb + k], 0, 0)

    seq_spec = pl.BlockSpec((1, ds, D_MODEL), lambda s, j, pt_ref: (s, 0, 0))
    page_spec = lambda k: pl.BlockSpec((1, PAGE_SIZE, D_MODEL), page(k))
    grid_spec = pltpu.PrefetchScalarGridSpec(
        num_scalar_prefetch=1,
        grid=(db, nblk),
        in_specs=[seq_spec, seq_spec, seq_spec, page_spec(0), page_spec(1), page_spec(0), page_spec(1),
                  pl.BlockSpec((1, rows, MOBA_BLOCK), lambda s, j, pt_ref: (j, 0, 0)),
                  pl.BlockSpec((rows, LANES), lambda s, j, pt_ref: (0, 0))],
        out_specs=seq_spec,
        scratch_shapes=[pltpu.VMEM((rows, D_MODEL), F32),
                        pltpu.VMEM((nblk, rows, D_MODEL), F32),
                        pltpu.VMEM((nblk, rows, LANES), F32),
                        pltpu.VMEM((nblk, rows, LANES), F32),
                        pltpu.VMEM((nblk, rows, LANES), F32)],
    )
    return pl.pallas_call(
        functools.partial(_attn_sample_kernel, nblk=nblk, ds=ds),
        grid_spec=grid_spec,
        out_shape=jax.ShapeDtypeStruct((db, ds, D_MODEL), F32),
        compiler_params=_cparams("arbitrary", "arbitrary"),
        name="attn_sample",
    )(pt, q, kn, vn, ck, ck, cv, cv, bias_blk, bias_own)


def _proj_res_kernel(x_ref, a_ref, g_ref, w_ref, o_ref):
    o_ref[...] = x_ref[...] + g_ref[0] * _dot(a_ref[...].astype(BF), w_ref[...])


def _proj_res(x, a, g, w, t_len, tm):
    n = x.shape[0]
    g_a, g_s = _mod_operand(g, t_len, tm)
    tile = pl.BlockSpec((tm, D_MODEL), lambda i: (i, 0))
    return pl.pallas_call(
        _proj_res_kernel,
        grid=(n // tm,),
        in_specs=[tile, tile, g_s, pl.BlockSpec((D_MODEL, D_MODEL), lambda i: (0, 0))],
        out_specs=tile,
        out_shape=jax.ShapeDtypeStruct((n, D_MODEL), F32),
        compiler_params=_cparams("arbitrary"),
        name="proj_res",
    )(x, a, g_a, w)


def _final_kernel(x_ref, pt_ref, g2_ref, fg_ref, o_ref):
    x = x_ref[...] + g2_ref[0] * pt_ref[...].T
    o_ref[...] = _rms(x, fg_ref[...])


def _final(x, peer_t, g2, fg, t_len, tm):
    n = x.shape[0]
    g_a, g_s = _mod_operand(g2, t_len, tm)
    tile = pl.BlockSpec((tm, D_MODEL), lambda i: (i, 0))
    return pl.pallas_call(
        _final_kernel,
        grid=(n // tm,),
        in_specs=[tile, pl.BlockSpec((D_MODEL, tm), lambda i: (0, i)), g_s,
                  pl.BlockSpec((1, D_MODEL), lambda i: (0, 0))],
        out_specs=tile,
        out_shape=jax.ShapeDtypeStruct((n, D_MODEL), F32),
        compiler_params=_cparams("arbitrary"),
        name="final_norm",
    )(x, peer_t, g_a, fg)


def _t5_bucket(rel):
    n = jnp.maximum(rel, 0)
    max_exact = N_BUCKETS // 2
    nf = jnp.maximum(n, max_exact).astype(F32)
    large = max_exact + (jnp.log(nf / max_exact) / math.log(MAX_DISTANCE / max_exact)
                         * (N_BUCKETS - max_exact)).astype(jnp.int32)
    large = jnp.minimum(large, N_BUCKETS - 1)
    return jnp.where(n < max_exact, n, large)


def _bias_table(rel, rel_bias, causal):
    b = jnp.transpose(rel_bias[_t5_bucket(rel)], (2, 0, 1))
    if causal:
        b = jnp.where((rel >= 0)[None], b, NEG)
    return b


def _prompt_bias(rel_bias):
    t = jnp.arange(MOBA_BLOCK, dtype=jnp.int32)
    d = t[:, None] - t[None, :]
    own = _bias_table(d, rel_bias, True)
    prev = _bias_table(d + MOBA_BLOCK, rel_bias, False)
    pair = lambda b: b.reshape(N_HEADS // 2, 2 * MOBA_BLOCK, MOBA_BLOCK)
    far = rel_bias[_t5_bucket(jnp.int32(MOBA_BLOCK + 1))]
    return pair(own), pair(prev), far


def _sample_bias(rel_bias, ds, past, nblk):
    t = jnp.arange(ds, dtype=jnp.int32)
    l = jnp.arange(MOBA_BLOCK, dtype=jnp.int32)
    blocks = []
    for j in range(nblk):
        rel = past + t[:, None] - (j * MOBA_BLOCK + l[None, :])
        blocks.append(_bias_table(rel, rel_bias, False).reshape(N_HEADS * ds, MOBA_BLOCK))
    own = _bias_table(t[:, None] - t[None, :], rel_bias, True).reshape(N_HEADS * ds, ds)
    own = jnp.concatenate([own, jnp.full((N_HEADS * ds, LANES - ds), NEG, F32)], axis=1)
    return jnp.stack(blocks), own


def _gate_weights(w_s, b_s, chunk_len):
    w = jnp.where(jnp.tril(jnp.ones((chunk_len, chunk_len), bool)), w_s[:, :chunk_len, :chunk_len], 0)
    reps = CHUNK // chunk_len
    if reps > 1:
        eye = jnp.eye(reps, dtype=w.dtype)
        w = jnp.einsum("ab,gts->gatbs", eye, w).reshape(A_GROUPS, CHUNK, CHUNK)
    bias = jnp.tile(b_s[:, :chunk_len].T, (reps, 1))
    bias = jnp.repeat(bias, D_MODEL // A_GROUPS, axis=1)
    return w.astype(BF), bias


def kernel(x_prompt, x_sample, cache_k, cache_v, page_table, c_prompt, c_sample, a_w_in, a_g_v, a_w_s, a_b_s, a_w_out, b_w_q, b_w_o, kv_g, w_kv, rel_bias, g_mix, g_ffn, w_ada, b_ada, p_w_q, p_sub_keys, p_w_u, p_w_v, final_g):
    batch, seq, _ = x_prompt.shape
    db, ds, _ = x_sample.shape
    n_p = batch * seq
    n_s = db * ds
    past = page_table.shape[1] * PAGE_SIZE
    row = lambda v: v.reshape(1, D_MODEL)

    w_in = a_w_in[0].astype(BF)
    w_out = a_w_out[0].astype(BF)
    wkv = w_kv.astype(BF)
    wq_attn = b_w_q[0].astype(BF)
    wo_attn = b_w_o[0].astype(BF)
    wqt = [p_w_q[l].T.astype(BF) for l in range(2)]
    sk = [p_sub_keys[l].astype(BF) for l in range(2)]
    wu = [p_w_u[l].astype(BF) for l in range(2)]
    wvt = [p_w_v[l].T.astype(BF) for l in range(2)]

    mod = _ada(jnp.concatenate([c_prompt, c_sample], axis=0), w_ada, b_ada)

    def mods(l, lo, hi):
        m = mod[l, lo:hi]
        return [m[:, k * D_MODEL:(k + 1) * D_MODEL] for k in range(6)]

    bias_own_p, bias_prev_p, c31 = _prompt_bias(rel_bias)
    nblk = past // MOBA_BLOCK
    bias_blk_s, bias_own_s = _sample_bias(rel_bias, ds, past, nblk)

    def trunk(x, t_len, lo, hi, gate_w, gate_b, tm, attend):
        sh1, sc1, g1, sh2, sc2, g2 = mods(0, lo, hi)
        x1, v_rows = _gmlp(x, row(g_mix[0]), sc1, sh1, g1, w_in, row(a_g_v[0]), gate_w, gate_b, w_out,
                           t_len, tm)
        peer_t = _peer(x1, row(g_ffn[0]), sc2, sh2, wqt[0], sk[0], wu[0], wvt[0], t_len)
        sh1b, sc1b, g1b, sh2b, sc2b, g2b = mods(1, lo, hi)
        x2, k, v, kb, vb, kmean, q = _kvq(x1, peer_t, g2, row(kv_g), wkv, row(g_mix[1]), sc1b, sh1b,
                                          wq_attn, t_len)
        att = attend(q, k, v, kb, vb, kmean)
        x3 = _proj_res(x2, att, g1b, wo_attn, t_len, MOBA_BLOCK)
        peer_t = _peer(x3, row(g_ffn[1]), sc2b, sh2b, wqt[1], sk[1], wu[1], wvt[1], t_len)
        y = _final(x3, peer_t, g2b, row(final_g), t_len, MOBA_BLOCK)
        return y, k, v, v_rows

    def attend_prompt(q, k, v, kb, vb, kmean):
        return _attn_prompt(q, kb, vb, kmean, bias_own_p, bias_prev_p, c31, batch, seq)

    def attend_sample(q, k, v, kb, vb, kmean):
        out = _attn_sample(q.reshape(db, ds, D_MODEL), k.reshape(db, ds, D_MODEL), v.reshape(db, ds, D_MODEL),
                           cache_k, cache_v, page_table, bias_blk_s, bias_own_s)
        return out.reshape(n_s, D_MODEL)

    gw_p, gb_p = _gate_weights(a_w_s[0], a_b_s[0], min(seq, CHUNK))
    gw_s, gb_s = _gate_weights(a_w_s[0], a_b_s[0], min(ds, CHUNK))

    y_p, k_p, v_p, _ = trunk(x_prompt.reshape(n_p, D_MODEL), seq, 0, batch, gw_p, gb_p, 512, attend_prompt)
    y_s, k_s, v_s, cv_s = trunk(x_sample.reshape(n_s, D_MODEL), ds, batch, batch + db, gw_s, gb_s, 128,
                                attend_sample)

    heads = lambda a, b, t: a.reshape(b, t, N_HEADS, HEAD_DIM)
    return (y_p.reshape(batch, seq, D_MODEL), y_s.reshape(db, ds, D_MODEL),
            heads(k_p, batch, seq), heads(v_p, batch, seq),
            heads(k_s, db, ds), heads(v_s, db, ds),
            cv_s.reshape(1, db, ds, D_MODEL))
```

```python
import functools
import math

import jax
import jax.numpy as jnp
from jax import lax
from jax.experimental import pallas as pl
from jax.experimental.pallas import tpu as pltpu

D_MODEL = 1024
N_HEADS = 16
HEAD_DIM = 64
CHUNK = 128
A_GROUPS = 8
MOBA_BLOCK = 256
MOBA_TOPK = 3
PAGE_SIZE = 128
N_BUCKETS = 32
MAX_DISTANCE = 128
PEER_HEADS = 8
PEER_NKEYS = 128
PEER_TOPK = 16
EPS = 1e-6
NEG = -1e30
NINF = float("-inf")
SCALE = HEAD_DIM ** -0.5

BF = jnp.bfloat16
F32 = jnp.float32

LANES = 128
VMEM_LIMIT = 56 * 1024 * 1024

_NT = (((1,), (1,)), ((), ()))


def _cparams(*sem):
    return pltpu.CompilerParams(dimension_semantics=sem, vmem_limit_bytes=VMEM_LIMIT)


def _rms(x, g):
    ms = jnp.mean(x * x, axis=-1, keepdims=True)
    return x * lax.rsqrt(ms + EPS) * g


def _dot(a, b):
    return jnp.dot(a, b, preferred_element_type=F32)


def _dot_nt(a, b):
    return lax.dot_general(a, b, _NT, preferred_element_type=F32)


def _ada_kernel(c_ref, w_ref, b_ref, o_ref):
    c = c_ref[...]
    s = (c * jax.nn.sigmoid(c)).astype(BF)
    o_ref[0] = _dot(s, w_ref[0].astype(BF)) + b_ref[0]


def _ada(c_all, w_ada, b_ada):
    m = c_all.shape[0]
    depth, _, n6 = w_ada.shape
    tn = 1536
    return pl.pallas_call(
        _ada_kernel,
        grid=(depth, n6 // tn),
        in_specs=[
            pl.BlockSpec((m, D_MODEL), lambda l, j: (0, 0)),
            pl.BlockSpec((1, D_MODEL, tn), lambda l, j: (l, 0, j)),
            pl.BlockSpec((1, 1, tn), lambda l, j: (l, 0, j)),
        ],
        out_specs=pl.BlockSpec((1, m, tn), lambda l, j: (l, 0, j)),
        out_shape=jax.ShapeDtypeStruct((depth, m, n6), F32),
        compiler_params=_cparams("arbitrary", "arbitrary"),
        name="ada",
    )(c_all, w_ada, b_ada.reshape(depth, 1, n6))


def _mod_operand(arr, t_len, tm):
    b = arr.shape[0]
    if t_len % tm == 0:
        per = t_len // tm
        return arr.reshape(b, 1, D_MODEL), pl.BlockSpec((1, 1, D_MODEL), lambda i, *_: (i // per, 0, 0))
    n = b * t_len
    rep = jnp.repeat(arr, t_len, axis=0).reshape(n // tm, tm, D_MODEL)
    return rep, pl.BlockSpec((1, tm, D_MODEL), lambda i, *_: (i, 0, 0))


def _gmlp_kernel(x_ref, gm_ref, sc_ref, sh_ref, g1_ref, win_ref, gv_ref, wg_ref, bs_ref, wout_ref,
                 xo_ref, v_ref, us_scr, *, tm):
    x = x_ref[...]
    h = _rms(x, gm_ref[...]) * (1.0 + sc_ref[0]) + sh_ref[0]
    uv = jax.nn.gelu(_dot(h.astype(BF), win_ref[...]))
    u = uv[:, :D_MODEL]
    v = _rms(uv[:, D_MODEL:], gv_ref[...])
    v_ref[...] = v
    vb = v.astype(BF)
    for c in range(tm // CHUNK):
        r0 = c * CHUNK
        parts = []
        for g in range(A_GROUPS):
            c0 = g * CHUNK
            parts.append(_dot(wg_ref[g], vb[r0:r0 + CHUNK, c0:c0 + CHUNK]))
        s = jnp.concatenate(parts, axis=1) + bs_ref[...]
        us_scr[r0:r0 + CHUNK, :] = (u[r0:r0 + CHUNK, :] * s).astype(BF)
    out = _dot(us_scr[...], wout_ref[...])
    xo_ref[...] = x + g1_ref[0] * out


def _gmlp(x, gm, sc, sh, g1, w_in, g_v, wg, bs, w_out, t_len, tm):
    n = x.shape[0]
    sc_a, sc_s = _mod_operand(sc, t_len, tm)
    sh_a, sh_s = _mod_operand(sh, t_len, tm)
    g1_a, g1_s = _mod_operand(g1, t_len, tm)
    full = lambda shape: pl.BlockSpec(shape, lambda i: (0,) * len(shape))
    return pl.pallas_call(
        functools.partial(_gmlp_kernel, tm=tm),
        grid=(n // tm,),
        in_specs=[
            pl.BlockSpec((tm, D_MODEL), lambda i: (i, 0)),
            full((1, D_MODEL)), sc_s, sh_s, g1_s,
            full((D_MODEL, 2 * D_MODEL)), full((1, D_MODEL)),
            full((A_GROUPS, CHUNK, CHUNK)), full((CHUNK, D_MODEL)),
            full((D_MODEL, D_MODEL)),
        ],
        out_specs=[pl.BlockSpec((tm, D_MODEL), lambda i: (i, 0)),
                   pl.BlockSpec((tm, D_MODEL), lambda i: (i, 0))],
        out_shape=[jax.ShapeDtypeStruct((n, D_MODEL), F32), jax.ShapeDtypeStruct((n, D_MODEL), F32)],
        scratch_shapes=[pltpu.VMEM((tm, D_MODEL), BF)],
        compiler_params=_cparams("arbitrary"),
        name="gmlp",
    )(x, gm, sc_a, sh_a, g1_a, w_in, g_v, wg, bs, w_out)


def _top16_rows(blk, out_scr):
    s = blk
    for k in range(PEER_TOPK):
        m = jnp.max(s, axis=0, keepdims=True)
        out_scr[k:k + 1, :] = m
        if k + 1 < PEER_TOPK:
            s = jnp.where(s == m, NINF, s)


def _kth_sum(a, b):
    row = lax.broadcasted_iota(jnp.int32, (8, a.shape[1]), 0)
    cands = [a[0:8] + b[0:1], a[8:16] + b[0:1]]
    for l in range(1, 8):
        kmax = PEER_TOPK // (l + 1)
        c = a[0:8] + b[l:l + 1]
        if kmax < 8:
            c = jnp.where(row < kmax, c, NINF)
        cands.append(c)
    cands.append(b[8:16] + a[0:1])
    m0 = None
    z = None
    m = None
    for k in range(PEER_TOPK):
        m = functools.reduce(jnp.maximum, cands)
        m = jnp.max(m, axis=0, keepdims=True)
        if k == 0:
            m0 = m
            z = jnp.ones_like(m)
        else:
            z = z + jnp.exp(m - m0)
        if k + 1 < PEER_TOPK:
            cands = [jnp.where(c == m, NINF, c) for c in cands]
    return m, m0, z


def _route_kernel(x_ref, gm_ref, sc_ref, sh_ref, wqt_ref, sk_ref,
                  ht_ref, s0_ref, s1_ref, e0_ref, e1_ref, thr_ref,
                  qt_scr, a_scr, b_scr, *, tn):
    x = x_ref[...]
    h = _rms(x, gm_ref[...]) * (1.0 + sc_ref[0]) + sh_ref[0]
    ht = h.T.astype(BF)
    ht_ref[...] = ht
    qt_scr[...] = _dot(wqt_ref[...], ht).astype(BF)

    def head(hd, carry):
        st = []
        for p in range(2):
            r0 = pl.multiple_of((hd * 2 + p) * PEER_NKEYS, PEER_NKEYS)
            st.append(_dot(sk_ref[p, hd], qt_scr[pl.ds(r0, PEER_NKEYS), :]))
        for tb in range(tn // LANES):
            lanes = slice(tb * LANES, (tb + 1) * LANES)
            s0 = st[0][:, lanes]
            s1 = st[1][:, lanes]
            s0_ref[tb, hd] = s0
            s1_ref[tb, hd] = s1
            _top16_rows(s0, a_scr)
            _top16_rows(s1, b_scr)
            a = a_scr[...]
            b = b_scr[...]
            thr, _, z = _kth_sum(a, b)
            thr_ref[tb, hd] = thr
            e0_ref[tb, hd] = jnp.exp(s0 - a[0:1]) * (1.0 / z)
            e1_ref[tb, hd] = jnp.exp(s1 - b[0:1])
        return carry

    lax.fori_loop(0, PEER_HEADS, head, 0)


def _route(x, gm, sc, sh, wqt, sk, layer, t_len, tn):
    n = x.shape[0]
    sc_a, sc_s = _mod_operand(sc, t_len, tn)
    sh_a, sh_s = _mod_operand(sh, t_len, tn)
    full = lambda shape: pl.BlockSpec(shape, lambda i: (0,) * len(shape))
    nlb = tn // LANES
    sshape = jax.ShapeDtypeStruct((n // LANES, PEER_HEADS, PEER_NKEYS, LANES), F32)
    sspec = pl.BlockSpec((nlb, PEER_HEADS, PEER_NKEYS, LANES), lambda i: (i, 0, 0, 0))
    return pl.pallas_call(
        functools.partial(_route_kernel, tn=tn),
        grid=(n // tn,),
        in_specs=[
            pl.BlockSpec((tn, D_MODEL), lambda i: (i, 0)),
            full((1, D_MODEL)), sc_s, sh_s,
            pl.BlockSpec((None, 2 * PEER_HEADS * PEER_NKEYS, D_MODEL), lambda i: (layer, 0, 0)),
            pl.BlockSpec((None, 2, PEER_HEADS, PEER_NKEYS, PEER_NKEYS), lambda i: (layer, 0, 0, 0, 0)),
        ],
        out_specs=[pl.BlockSpec((D_MODEL, tn), lambda i: (0, i)), sspec, sspec, sspec, sspec,
                   pl.BlockSpec((nlb, PEER_HEADS, 1, LANES), lambda i: (i, 0, 0, 0))],
        out_shape=[jax.ShapeDtypeStruct((D_MODEL, n), BF), sshape, sshape, sshape, sshape,
                   jax.ShapeDtypeStruct((n // LANES, PEER_HEADS, 1, LANES), F32)],
        scratch_shapes=[pltpu.VMEM((2 * PEER_HEADS * PEER_NKEYS, tn), BF),
                        pltpu.VMEM((PEER_TOPK, LANES), F32), pltpu.VMEM((PEER_TOPK, LANES), F32)],
        compiler_params=_cparams("arbitrary"),
        name="peer_route",
    )(x, gm, sc_a, sh_a, wqt, sk)


PEER_I_PER_STEP = 8


def _expert_kernel(ht_ref, s1_ref, e1_ref, s0_ref, e0_ref, thr_ref, wu_ref, wvt_ref, o_ref,
                   acc_scr, ga_scr, a_scr, *, tn):
    e = pl.program_id(1)

    @pl.when(e == 0)
    def _():
        acc_scr[...] = jnp.zeros_like(acc_scr)

    a_all = _dot(wu_ref[...], ht_ref[...])
    jr = 16
    bcast = lambda r: jnp.broadcast_to(r, (jr, LANES))
    for tb in range(tn // LANES):
        lanes = slice(tb * LANES, (tb + 1) * LANES)
        a_scr[tb] = a_all[:, lanes]

        for jg in range(PEER_NKEYS // jr):
            js = slice(jg * jr, (jg + 1) * jr)
            g = [None] * PEER_I_PER_STEP
            for hd in range(PEER_HEADS):
                s1 = s1_ref[tb, hd, js, :]
                e1 = e1_ref[tb, hd, js, :]
                thr = bcast(thr_ref[tb, hd])
                for il in range(PEER_I_PER_STEP):
                    val = s1 + bcast(s0_ref[tb, hd, il:il + 1, :])
                    w = e1 * bcast(e0_ref[tb, hd, il:il + 1, :])
                    t = jnp.where(val >= thr, w, 0.0)
                    g[il] = t if g[il] is None else g[il] + t
            for il in range(PEER_I_PER_STEP):
                r0 = il * PEER_NKEYS + jg * jr
                ga_scr[r0:r0 + jr, lanes] = (jax.nn.gelu(a_scr[tb, r0:r0 + jr, :]) * g[il]).astype(BF)
    acc_scr[...] += _dot(wvt_ref[...], ga_scr[...])

    @pl.when(e == pl.num_programs(1) - 1)
    def _():
        o_ref[...] = acc_scr[...]


def _experts(ht, s0, s1, e0, e1, thr, wu, wvt, layer, tn):
    n = ht.shape[1]
    n_exp = wu.shape[1]
    te = PEER_I_PER_STEP * PEER_NKEYS
    nlb = tn // LANES
    return pl.pallas_call(
        functools.partial(_expert_kernel, tn=tn),
        grid=(n // tn, n_exp // te),
        in_specs=[
            pl.BlockSpec((D_MODEL, tn), lambda t, e: (0, t)),
            pl.BlockSpec((nlb, PEER_HEADS, PEER_NKEYS, LANES), lambda t, e: (t, 0, 0, 0)),
            pl.BlockSpec((nlb, PEER_HEADS, PEER_NKEYS, LANES), lambda t, e: (t, 0, 0, 0)),
            pl.BlockSpec((nlb, PEER_HEADS, PEER_I_PER_STEP, LANES), lambda t, e: (t, 0, e, 0)),
            pl.BlockSpec((nlb, PEER_HEADS, PEER_I_PER_STEP, LANES), lambda t, e: (t, 0, e, 0)),
            pl.BlockSpec((nlb, PEER_HEADS, 1, LANES), lambda t, e: (t, 0, 0, 0)),
            pl.BlockSpec((None, te, D_MODEL), lambda t, e: (layer, e, 0)),
            pl.BlockSpec((None, D_MODEL, te), lambda t, e: (layer, 0, e)),
        ],
        out_specs=pl.BlockSpec((D_MODEL, tn), lambda t, e: (0, t)),
        out_shape=jax.ShapeDtypeStruct((D_MODEL, n), F32),
        scratch_shapes=[pltpu.VMEM((D_MODEL, tn), F32), pltpu.VMEM((te, tn), BF),
                        pltpu.VMEM((nlb, te, LANES), F32)],
        compiler_params=_cparams("arbitrary", "arbitrary"),
        name="peer_experts",
    )(ht, s1, e1, s0, e0, thr, wu, wvt)


def _peer(x, gm, sc, sh, wqt, sk, wu, wvt, layer, t_len):
    n = x.shape[0]
    ht, s0, s1, e0, e1, thr = _route(x, gm, sc, sh, wqt, sk, layer, t_len, 256)
    return _experts(ht, s0, s1, e0, e1, thr, wu, wvt, layer, min(512, n))


def _kvq_kernel(x_ref, pt_ref, g2_ref, kvg_ref, wkv_ref, gm_ref, sc_ref, sh_ref, wq_ref,
                xo_ref, k_ref, v_ref, kb_ref, vb_ref, km_ref, q_ref):
    x = x_ref[...] + g2_ref[0] * pt_ref[...].T
    xo_ref[...] = x
    kv = _dot(_rms(x, kvg_ref[...]).astype(BF), wkv_ref[...])
    k = kv[:, :D_MODEL]
    v = kv[:, D_MODEL:]
    k_ref[...] = k
    v_ref[...] = v
    kb_ref[...] = k.astype(BF)
    vb_ref[...] = v.astype(BF)
    km_ref[0] = jnp.mean(k, axis=0, keepdims=True)
    h = _rms(x, gm_ref[...]) * (1.0 + sc_ref[0]) + sh_ref[0]
    q_ref[...] = _dot(h.astype(BF), wq_ref[...])


def _kvq(x, peer_t, g2, kv_g, w_kv, gm, sc, sh, w_q, t_len):
    n = x.shape[0]
    tm = MOBA_BLOCK
    g2_a, g2_s = _mod_operand(g2, t_len, tm)
    sc_a, sc_s = _mod_operand(sc, t_len, tm)
    sh_a, sh_s = _mod_operand(sh, t_len, tm)
    full = lambda shape: pl.BlockSpec(shape, lambda i: (0,) * len(shape))
    tile = pl.BlockSpec((tm, D_MODEL), lambda i: (i, 0))
    f32 = jax.ShapeDtypeStruct((n, D_MODEL), F32)
    bf = jax.ShapeDtypeStruct((n, D_MODEL), BF)
    return pl.pallas_call(
        _kvq_kernel,
        grid=(n // tm,),
        in_specs=[tile, pl.BlockSpec((D_MODEL, tm), lambda i: (0, i)), g2_s,
                  full((1, D_MODEL)), full((D_MODEL, 2 * D_MODEL)),
                  full((1, D_MODEL)), sc_s, sh_s, full((D_MODEL, D_MODEL))],
        out_specs=[tile, tile, tile, tile, tile,
                   pl.BlockSpec((1, 1, D_MODEL), lambda i: (i, 0, 0)), tile],
        out_shape=[f32, f32, f32, bf, bf, jax.ShapeDtypeStruct((n // tm, 1, D_MODEL), F32), f32],
        compiler_params=_cparams("arbitrary"),
        name="kvq",
    )(x, peer_t, g2_a, kv_g, w_kv, gm, sc_a, sh_a, w_q)


def _third_largest(g):
    m1 = jnp.max(g, axis=1, keepdims=True)
    g2 = jnp.where(g == m1, NINF, g)
    m2 = jnp.max(g2, axis=1, keepdims=True)
    g3 = jnp.where(g2 == m2, NINF, g2)
    return jnp.max(g3, axis=1, keepdims=True)


def _attn_prompt_kernel(c31_ref, q_ref, k_ref, v_ref, km_ref, bo_ref, bp_ref, o_ref, *, nb):
    hp = pl.program_id(1)
    tq = MOBA_BLOCK
    lane = lax.broadcasted_iota(jnp.int32, (tq, LANES), 1)
    row = lax.broadcasted_iota(jnp.int32, (2 * tq, 1), 0)
    c31 = jnp.where(row < tq, c31_ref[2 * hp], c31_ref[2 * hp + 1])
    km = jnp.concatenate([km_ref[0], jnp.zeros((LANES - nb, LANES), F32)], axis=0)
    jidx = lax.broadcasted_iota(jnp.int32, (2 * tq, LANES), 1)

    for i in range(nb):
        q2 = q_ref[i * tq:(i + 1) * tq, :]
        qf = jnp.concatenate([jnp.where(lane < HEAD_DIM, q2, 0.0),
                              jnp.where(lane >= HEAD_DIM, q2, 0.0)], axis=0)
        qs = qf.astype(BF)

        def scores(j):
            return _dot_nt(qs, k_ref[j * tq:(j + 1) * tq, :]) * SCALE

        pieces = []
        if i > MOBA_TOPK:
            gate = lax.dot_general(qf, km, _NT, precision=lax.Precision.HIGHEST,
                                   preferred_element_type=F32)
            gm = jnp.where(jidx < i, gate, NINF)
            selpen = jnp.where(gm >= _third_largest(gm), 0.0, NEG)
            pen = lambda j: selpen[:, j:j + 1]
        else:
            pen = lambda j: 0.0
        for j in range(i - 1):
            pieces.append(scores(j) + (c31 + pen(j)))
        if i >= 1:
            pieces.append(scores(i - 1) + (bp_ref[0] + pen(i - 1)))
        pieces.append(scores(i) + bo_ref[0])

        m = functools.reduce(jnp.maximum, [jnp.max(s, axis=1, keepdims=True) for s in pieces])
        l = None
        acc = None
        for j, s in enumerate(pieces):
            p = jnp.exp(s - m)
            ls = jnp.sum(p, axis=1, keepdims=True)
            o = _dot(p.astype(BF), v_ref[j * tq:(j + 1) * tq, :])
            l = ls if l is None else l + ls
            acc = o if acc is None else acc + o
        out = acc / l
        o_ref[i * tq:(i + 1) * tq, :] = jnp.where(lane < HEAD_DIM, out[:tq], out[tq:]).astype(o_ref.dtype)


def _attn_prompt(q, kb, vb, kmean, bias_own, bias_prev, c31, batch, seq):
    n = q.shape[0]
    nb = seq // MOBA_BLOCK
    tq = MOBA_BLOCK
    grid_spec = pltpu.PrefetchScalarGridSpec(
        num_scalar_prefetch=1,
        grid=(batch, N_HEADS // 2),
        in_specs=[
            pl.BlockSpec((seq, LANES), lambda b, hp, c: (b, hp)),
            pl.BlockSpec((seq, LANES), lambda b, hp, c: (b, hp)),
            pl.BlockSpec((seq, LANES), lambda b, hp, c: (b, hp)),
            pl.BlockSpec((1, nb, LANES), lambda b, hp, c: (b, 0, hp)),
            pl.BlockSpec((1, 2 * tq, tq), lambda b, hp, c: (hp, 0, 0)),
            pl.BlockSpec((1, 2 * tq, tq), lambda b, hp, c: (hp, 0, 0)),
        ],
        out_specs=pl.BlockSpec((seq, LANES), lambda b, hp, c: (b, hp)),
    )
    return pl.pallas_call(
        functools.partial(_attn_prompt_kernel, nb=nb),
        grid_spec=grid_spec,
        out_shape=jax.ShapeDtypeStruct((n, D_MODEL), BF),
        compiler_params=_cparams("arbitrary", "arbitrary"),
        name="attn_prompt",
    )(c31, q, kb, vb, kmean.reshape(batch, nb, D_MODEL), bias_own, bias_prev)


SAMPLE_BLOCKS_PER_STEP = 4


def _attn_sample_kernel(pt_ref, q_ref, kn_ref, vn_ref, *refs, nblk, ds, ppb):
    bps = SAMPLE_BLOCKS_PER_STEP
    npg = bps * ppb
    k_refs = refs[:npg]
    v_refs = refs[npg:2 * npg]
    bias_ref, bown_ref, o_ref, qbd, o_scr, m_scr, l_scr, g_scr = refs[2 * npg:]
    step = pl.program_id(1)
    rows = N_HEADS * ds
    row = lax.broadcasted_iota(jnp.int32, (rows, D_MODEL), 0)
    col = lax.broadcasted_iota(jnp.int32, (rows, D_MODEL), 1)
    own = (col // HEAD_DIM) == (row // ds)

    @pl.when(step == 0)
    def _():
        qt = jnp.concatenate([q_ref[0]] * N_HEADS, axis=0)
        qbd[...] = jnp.where(own, qt, 0.0)

    qf = qbd[...]
    qb = qf.astype(BF)
    for bl in range(bps):
        blk = step * bps + bl
        ks = [k_refs[bl * ppb + pg][0] for pg in range(ppb)]
        vs = [v_refs[bl * ppb + pg][0] for pg in range(ppb)]
        kmean = functools.reduce(jnp.add, [jnp.sum(k, axis=0, keepdims=True) for k in ks]) * (1.0 / MOBA_BLOCK)
        gate = jnp.sum(qf * kmean, axis=1, keepdims=True)
        s = jnp.concatenate([_dot_nt(qb, k.astype(BF)) for k in ks], axis=1) * SCALE + bias_ref[bl]
        m = jnp.max(s, axis=1, keepdims=True)
        p = jnp.exp(s - m)
        pb = p.astype(BF)
        o_scr[blk] = functools.reduce(jnp.add, [
            _dot(pb[:, pg * PAGE_SIZE:(pg + 1) * PAGE_SIZE], vs[pg].astype(BF)) for pg in range(ppb)])
        m_scr[blk] = jnp.broadcast_to(m, (rows, LANES))
        l_scr[blk] = jnp.broadcast_to(jnp.sum(p, axis=1, keepdims=True), (rows, LANES))
        g_scr[blk] = jnp.broadcast_to(gate, (rows, LANES))

    @pl.when(step == pl.num_programs(1) - 1)
    def _():
        pad = jnp.zeros((LANES - ds, D_MODEL), F32)
        knb = jnp.concatenate([kn_ref[0], pad], axis=0).astype(BF)
        vnb = jnp.concatenate([vn_ref[0], pad], axis=0).astype(BF)
        so = _dot_nt(qb, knb) * SCALE + bown_ref[...]
        mo = jnp.max(so, axis=1, keepdims=True)
        po = jnp.exp(so - mo)
        lo = jnp.broadcast_to(jnp.sum(po, axis=1, keepdims=True), (rows, LANES))
        mo = jnp.broadcast_to(mo, (rows, LANES))
        oo = _dot(po.astype(BF), vnb)

        gs = [g_scr[b] for b in range(nblk)]
        g1 = functools.reduce(jnp.maximum, gs)
        gs2 = [jnp.where(g == g1, NINF, g) for g in gs]
        g2 = functools.reduce(jnp.maximum, gs2)
        gs3 = [jnp.where(g == g2, NINF, g) for g in gs2]
        g3 = functools.reduce(jnp.maximum, gs3)
        sel = [g >= g3 for g in gs]

        mm = mo
        for b in range(nblk):
            mm = jnp.maximum(mm, jnp.where(sel[b], m_scr[b], NINF))
        wide = lambda w: jnp.concatenate([w] * (D_MODEL // LANES), axis=1)
        wo = jnp.exp(mo - mm)
        lsum = lo * wo
        acc = oo * wide(wo)
        for b in range(nblk):
            wb = jnp.where(sel[b], jnp.exp(m_scr[b] - mm), 0.0)
            lsum = lsum + l_scr[b] * wb
            acc = acc + o_scr[b] * wide(wb)
        res = jnp.where(own, acc * wide(1.0 / lsum), 0.0)
        out = res[0:ds]
        for hh in range(1, N_HEADS):
            out = out + res[hh * ds:(hh + 1) * ds]
        o_ref[0] = out


def _attn_sample(q, kn, vn, cache_k, cache_v, page_table, bias_blk, bias_own):
    db, ds, _ = q.shape
    n_pages = page_table.shape[1]
    ppb = MOBA_BLOCK // PAGE_SIZE
    nblk = n_pages // ppb
    bps = SAMPLE_BLOCKS_PER_STEP
    npg = bps * ppb
    rows = N_HEADS * ds
    n_pool = cache_k.shape[0]
    ck = cache_k.reshape(n_pool, PAGE_SIZE, D_MODEL)
    cv = cache_v.reshape(n_pool, PAGE_SIZE, D_MODEL)
    pt = page_table.reshape(-1)

    def page_spec(k):
        return pl.BlockSpec((1, PAGE_SIZE, D_MODEL),
                            lambda s, j, pt_ref: (pt_ref[s * n_pages + j * npg + k], 0, 0))

    seq_spec = pl.BlockSpec((1, ds, D_MODEL), lambda s, j, pt_ref: (s, 0, 0))
    pages = [page_spec(k) for k in range(npg)]
    grid_spec = pltpu.PrefetchScalarGridSpec(
        num_scalar_prefetch=1,
        grid=(db, nblk // bps),
        in_specs=[seq_spec, seq_spec, seq_spec] + pages + pages + [
            pl.BlockSpec((bps, rows, MOBA_BLOCK), lambda s, j, pt_ref: (j, 0, 0)),
            pl.BlockSpec((rows, LANES), lambda s, j, pt_ref: (0, 0))],
        out_specs=seq_spec,
        scratch_shapes=[pltpu.VMEM((rows, D_MODEL), F32),
                        pltpu.VMEM((nblk, rows, D_MODEL), F32),
                        pltpu.VMEM((nblk, rows, LANES), F32),
                        pltpu.VMEM((nblk, rows, LANES), F32),
                        pltpu.VMEM((nblk, rows, LANES), F32)],
    )
    return pl.pallas_call(
        functools.partial(_attn_sample_kernel, nblk=nblk, ds=ds, ppb=ppb),
        grid_spec=grid_spec,
        out_shape=jax.ShapeDtypeStruct((db, ds, D_MODEL), F32),
        compiler_params=_cparams("arbitrary", "arbitrary"),
        name="attn_sample",
    )(pt, q, kn, vn, *([ck] * npg), *([cv] * npg), bias_blk, bias_own)


def _proj_res_kernel(x_ref, a_ref, g_ref, w_ref, o_ref):
    o_ref[...] = x_ref[...] + g_ref[0] * _dot(a_ref[...].astype(BF), w_ref[...])


def _proj_res(x, a, g, w, t_len, tm):
    n = x.shape[0]
    g_a, g_s = _mod_operand(g, t_len, tm)
    tile = pl.BlockSpec((tm, D_MODEL), lambda i: (i, 0))
    return pl.pallas_call(
        _proj_res_kernel,
        grid=(n // tm,),
        in_specs=[tile, tile, g_s, pl.BlockSpec((D_MODEL, D_MODEL), lambda i: (0, 0))],
        out_specs=tile,
        out_shape=jax.ShapeDtypeStruct((n, D_MODEL), F32),
        compiler_params=_cparams("arbitrary"),
        name="proj_res",
    )(x, a, g_a, w)


def _final_kernel(x_ref, pt_ref, g2_ref, fg_ref, o_ref):
    x = x_ref[...] + g2_ref[0] * pt_ref[...].T
    o_ref[...] = _rms(x, fg_ref[...])


def _final(x, peer_t, g2, fg, t_len, tm):
    n = x.shape[0]
    g_a, g_s = _mod_operand(g2, t_len, tm)
    tile = pl.BlockSpec((tm, D_MODEL), lambda i: (i, 0))
    return pl.pallas_call(
        _final_kernel,
        grid=(n // tm,),
        in_specs=[tile, pl.BlockSpec((D_MODEL, tm), lambda i: (0, i)), g_s,
                  pl.BlockSpec((1, D_MODEL), lambda i: (0, 0))],
        out_specs=tile,
        out_shape=jax.ShapeDtypeStruct((n, D_MODEL), F32),
        compiler_params=_cparams("arbitrary"),
        name="final_norm",
    )(x, peer_t, g_a, fg)


def _transpose_kernel(x_ref, o_ref):
    o_ref[0] = x_ref[0].T.astype(BF)


def _transpose_bf16(x, tr=512):
    nl, r, c = x.shape
    return pl.pallas_call(
        _transpose_kernel,
        grid=(nl, r // tr),
        in_specs=[pl.BlockSpec((1, tr, c), lambda l, i: (l, i, 0))],
        out_specs=pl.BlockSpec((1, c, tr), lambda l, i: (l, 0, i)),
        out_shape=jax.ShapeDtypeStruct((nl, c, r), BF),
        compiler_params=_cparams("arbitrary", "arbitrary"),
        name="transpose_bf16",
    )(x)


def _t5_bucket(rel):
    n = jnp.maximum(rel, 0)
    max_exact = N_BUCKETS // 2
    nf = jnp.maximum(n, max_exact).astype(F32)
    large = max_exact + (jnp.log(nf / max_exact) / math.log(MAX_DISTANCE / max_exact)
                         * (N_BUCKETS - max_exact)).astype(jnp.int32)
    large = jnp.minimum(large, N_BUCKETS - 1)
    return jnp.where(n < max_exact, n, large)


def _bias_table(rel, rel_bias, causal):
    onehot = (_t5_bucket(rel)[..., None] == jnp.arange(N_BUCKETS, dtype=jnp.int32)).astype(F32)
    b = jnp.einsum("rcb,bh->hrc", onehot, rel_bias, precision=lax.Precision.HIGHEST)
    if causal:
        b = jnp.where((rel >= 0)[None], b, NEG)
    return b


def _prompt_bias(rel_bias):
    t = jnp.arange(MOBA_BLOCK, dtype=jnp.int32)
    d = t[:, None] - t[None, :]
    own = _bias_table(d, rel_bias, True)
    prev = _bias_table(d + MOBA_BLOCK, rel_bias, False)
    pair = lambda b: b.reshape(N_HEADS // 2, 2 * MOBA_BLOCK, MOBA_BLOCK)
    far = rel_bias[_t5_bucket(jnp.int32(MOBA_BLOCK + 1))]
    return pair(own), pair(prev), far


def _sample_bias(rel_bias, ds, past, nblk):
    t = jnp.arange(ds, dtype=jnp.int32)
    l = jnp.arange(MOBA_BLOCK, dtype=jnp.int32)
    blocks = []
    for j in range(nblk):
        rel = past + t[:, None] - (j * MOBA_BLOCK + l[None, :])
        blocks.append(_bias_table(rel, rel_bias, False).reshape(N_HEADS * ds, MOBA_BLOCK))
    own = _bias_table(t[:, None] - t[None, :], rel_bias, True).reshape(N_HEADS * ds, ds)
    own = jnp.concatenate([own, jnp.full((N_HEADS * ds, LANES - ds), NEG, F32)], axis=1)
    return jnp.stack(blocks), own


def _gate_weights(w_s, b_s, chunk_len):
    w = jnp.where(jnp.tril(jnp.ones((chunk_len, chunk_len), bool)), w_s[:, :chunk_len, :chunk_len], 0)
    reps = CHUNK // chunk_len
    if reps > 1:
        eye = jnp.eye(reps, dtype=w.dtype)
        w = jnp.einsum("ab,gts->gatbs", eye, w).reshape(A_GROUPS, CHUNK, CHUNK)
    bias = jnp.tile(b_s[:, :chunk_len].T, (reps, 1))
    bias = jnp.repeat(bias, D_MODEL // A_GROUPS, axis=1)
    return w.astype(BF), bias


def kernel(x_prompt, x_sample, cache_k, cache_v, page_table, c_prompt, c_sample, a_w_in, a_g_v, a_w_s, a_b_s, a_w_out, b_w_q, b_w_o, kv_g, w_kv, rel_bias, g_mix, g_ffn, w_ada, b_ada, p_w_q, p_sub_keys, p_w_u, p_w_v, final_g):
    batch, seq, _ = x_prompt.shape
    db, ds, _ = x_sample.shape
    n_p = batch * seq
    n_s = db * ds
    past = page_table.shape[1] * PAGE_SIZE
    row = lambda v: v.reshape(1, D_MODEL)

    w_in = a_w_in[0].astype(BF)
    w_out = a_w_out[0].astype(BF)
    wkv = w_kv.astype(BF)
    wq_attn = b_w_q[0].astype(BF)
    wo_attn = b_w_o[0].astype(BF)
    wqt = _transpose_bf16(p_w_q)
    sk = p_sub_keys.astype(BF)
    wu = p_w_u.astype(BF)
    wvt = _transpose_bf16(p_w_v)

    mod = _ada(jnp.concatenate([c_prompt, c_sample], axis=0), w_ada, b_ada)

    def mods(l, lo, hi):
        m = mod[l, lo:hi]
        return [m[:, k * D_MODEL:(k + 1) * D_MODEL] for k in range(6)]

    bias_own_p, bias_prev_p, c31 = _prompt_bias(rel_bias)
    nblk = past // MOBA_BLOCK
    bias_blk_s, bias_own_s = _sample_bias(rel_bias, ds, past, nblk)

    def trunk(x, t_len, lo, hi, gate_w, gate_b, tm, attend):
        sh1, sc1, g1, sh2, sc2, g2 = mods(0, lo, hi)
        x1, v_rows = _gmlp(x, row(g_mix[0]), sc1, sh1, g1, w_in, row(a_g_v[0]), gate_w, gate_b, w_out,
                           t_len, tm)
        peer_t = _peer(x1, row(g_ffn[0]), sc2, sh2, wqt, sk, wu, wvt, 0, t_len)
        sh1b, sc1b, g1b, sh2b, sc2b, g2b = mods(1, lo, hi)
        x2, k, v, kb, vb, kmean, q = _kvq(x1, peer_t, g2, row(kv_g), wkv, row(g_mix[1]), sc1b, sh1b,
                                          wq_attn, t_len)
        att = attend(q, k, v, kb, vb, kmean)
        x3 = _proj_res(x2, att, g1b, wo_attn, t_len, MOBA_BLOCK)
        peer_t = _peer(x3, row(g_ffn[1]), sc2b, sh2b, wqt, sk, wu, wvt, 1, t_len)
        y = _final(x3, peer_t, g2b, row(final_g), t_len, MOBA_BLOCK)
        return y, k, v, v_rows

    def attend_prompt(q, k, v, kb, vb, kmean):
        return _attn_prompt(q, kb, vb, kmean, bias_own_p, bias_prev_p, c31, batch, seq)

    def attend_sample(q, k, v, kb, vb, kmean):
        out = _attn_sample(q.reshape(db, ds, D_MODEL), k.reshape(db, ds, D_MODEL), v.reshape(db, ds, D_MODEL),
                           cache_k, cache_v, page_table, bias_blk_s, bias_own_s)
        return out.reshape(n_s, D_MODEL)

    gw_p, gb_p = _gate_weights(a_w_s[0], a_b_s[0], min(seq, CHUNK))
    gw_s, gb_s = _gate_weights(a_w_s[0], a_b_s[0], min(ds, CHUNK))

    y_p, k_p, v_p, _ = trunk(x_prompt.reshape(n_p, D_MODEL), seq, 0, batch, gw_p, gb_p, 512, attend_prompt)
    y_s, k_s, v_s, cv_s = trunk(x_sample.reshape(n_s, D_MODEL), ds, batch, batch + db, gw_s, gb_s, 128,
                                attend_sample)

    heads = lambda a, b, t: a.reshape(b, t, N_HEADS, HEAD_DIM)
    return (y_p.reshape(batch, seq, D_MODEL), y_s.reshape(db, ds, D_MODEL),
            heads(k_p, batch, seq), heads(v_p, batch, seq),
            heads(k_s, db, ds), heads(v_s, db, ds),
            cv_s.reshape(1, db, ds, D_MODEL))
```

```python
import functools
import math

import jax
import jax.numpy as jnp
from jax import lax
from jax.experimental import pallas as pl
from jax.experimental.pallas import tpu as pltpu

D_MODEL = 1024
N_HEADS = 16
HEAD_DIM = 64
CHUNK = 128
A_GROUPS = 8
MOBA_BLOCK = 256
MOBA_TOPK = 3
PAGE_SIZE = 128
N_BUCKETS = 32
MAX_DISTANCE = 128
PEER_HEADS = 8
PEER_NKEYS = 128
PEER_TOPK = 16
EPS = 1e-6
NEG = -1e30
NINF = float("-inf")
SCALE = HEAD_DIM ** -0.5

BF = jnp.bfloat16
F32 = jnp.float32

LANES = 128
VMEM_LIMIT = 56 * 1024 * 1024

_NT = (((1,), (1,)), ((), ()))


def _cparams(*sem):
    return pltpu.CompilerParams(dimension_semantics=sem, vmem_limit_bytes=VMEM_LIMIT)


def _rms(x, g):
    ms = jnp.mean(x * x, axis=-1, keepdims=True)
    return x * lax.rsqrt(ms + EPS) * g


def _dot(a, b):
    return jnp.dot(a, b, preferred_element_type=F32)


def _dot_nt(a, b):
    return lax.dot_general(a, b, _NT, preferred_element_type=F32)


def _ada_kernel(c_ref, w_ref, b_ref, o_ref):
    c = c_ref[...]
    s = (c * jax.nn.sigmoid(c)).astype(BF)
    o_ref[0] = _dot(s, w_ref[0].astype(BF)) + b_ref[0]


def _ada(c_all, w_ada, b_ada):
    m = c_all.shape[0]
    depth, _, n6 = w_ada.shape
    tn = 1536
    return pl.pallas_call(
        _ada_kernel,
        grid=(depth, n6 // tn),
        in_specs=[
            pl.BlockSpec((m, D_MODEL), lambda l, j: (0, 0)),
            pl.BlockSpec((1, D_MODEL, tn), lambda l, j: (l, 0, j)),
            pl.BlockSpec((1, 1, tn), lambda l, j: (l, 0, j)),
        ],
        out_specs=pl.BlockSpec((1, m, tn), lambda l, j: (l, 0, j)),
        out_shape=jax.ShapeDtypeStruct((depth, m, n6), F32),
        compiler_params=_cparams("arbitrary", "arbitrary"),
        name="ada",
    )(c_all, w_ada, b_ada.reshape(depth, 1, n6))


def _mod_operand(arr, t_len, tm):
    b = arr.shape[0]
    if t_len % tm == 0:
        per = t_len // tm
        return arr.reshape(b, 1, D_MODEL), pl.BlockSpec((1, 1, D_MODEL), lambda i, *_: (i // per, 0, 0))
    n = b * t_len
    rep = jnp.repeat(arr, t_len, axis=0).reshape(n // tm, tm, D_MODEL)
    return rep, pl.BlockSpec((1, tm, D_MODEL), lambda i, *_: (i, 0, 0))


def _gmlp_kernel(x_ref, gm_ref, sc_ref, sh_ref, g1_ref, win_ref, gv_ref, wg_ref, bs_ref, wout_ref,
                 xo_ref, v_ref, us_scr, *, tm):
    x = x_ref[...]
    h = _rms(x, gm_ref[...]) * (1.0 + sc_ref[0]) + sh_ref[0]
    uv = jax.nn.gelu(_dot(h.astype(BF), win_ref[...]))
    u = uv[:, :D_MODEL]
    v = _rms(uv[:, D_MODEL:], gv_ref[...])
    v_ref[...] = v
    vb = v.astype(BF)
    for c in range(tm // CHUNK):
        r0 = c * CHUNK
        parts = []
        for g in range(A_GROUPS):
            c0 = g * CHUNK
            parts.append(_dot(wg_ref[g], vb[r0:r0 + CHUNK, c0:c0 + CHUNK]))
        s = jnp.concatenate(parts, axis=1) + bs_ref[...]
        us_scr[r0:r0 + CHUNK, :] = (u[r0:r0 + CHUNK, :] * s).astype(BF)
    out = _dot(us_scr[...], wout_ref[...])
    xo_ref[...] = x + g1_ref[0] * out


def _gmlp(x, gm, sc, sh, g1, w_in, g_v, wg, bs, w_out, t_len, tm):
    n = x.shape[0]
    sc_a, sc_s = _mod_operand(sc, t_len, tm)
    sh_a, sh_s = _mod_operand(sh, t_len, tm)
    g1_a, g1_s = _mod_operand(g1, t_len, tm)
    full = lambda shape: pl.BlockSpec(shape, lambda i: (0,) * len(shape))
    return pl.pallas_call(
        functools.partial(_gmlp_kernel, tm=tm),
        grid=(n // tm,),
        in_specs=[
            pl.BlockSpec((tm, D_MODEL), lambda i: (i, 0)),
            full((1, D_MODEL)), sc_s, sh_s, g1_s,
            full((D_MODEL, 2 * D_MODEL)), full((1, D_MODEL)),
            full((A_GROUPS, CHUNK, CHUNK)), full((CHUNK, D_MODEL)),
            full((D_MODEL, D_MODEL)),
        ],
        out_specs=[pl.BlockSpec((tm, D_MODEL), lambda i: (i, 0)),
                   pl.BlockSpec((tm, D_MODEL), lambda i: (i, 0))],
        out_shape=[jax.ShapeDtypeStruct((n, D_MODEL), F32), jax.ShapeDtypeStruct((n, D_MODEL), F32)],
        scratch_shapes=[pltpu.VMEM((tm, D_MODEL), BF)],
        compiler_params=_cparams("arbitrary"),
        name="gmlp",
    )(x, gm, sc_a, sh_a, g1_a, w_in, g_v, wg, bs, w_out)


def _top16_rows(blk, out_scr):
    s = blk
    for k in range(PEER_TOPK):
        m = jnp.max(s, axis=0, keepdims=True)
        out_scr[k:k + 1, :] = m
        if k + 1 < PEER_TOPK:
            s = jnp.where(s == m, NINF, s)


def _kth_sum(a, b):
    row = lax.broadcasted_iota(jnp.int32, (8, a.shape[1]), 0)
    cands = [a[0:8] + b[0:1], a[8:16] + b[0:1]]
    for l in range(1, 8):
        kmax = PEER_TOPK // (l + 1)
        c = a[0:8] + b[l:l + 1]
        if kmax < 8:
            c = jnp.where(row < kmax, c, NINF)
        cands.append(c)
    cands.append(b[8:16] + a[0:1])
    m0 = None
    z = None
    m = None
    for k in range(PEER_TOPK):
        m = functools.reduce(jnp.maximum, cands)
        m = jnp.max(m, axis=0, keepdims=True)
        if k == 0:
            m0 = m
            z = jnp.ones_like(m)
        else:
            z = z + jnp.exp(m - m0)
        if k + 1 < PEER_TOPK:
            cands = [jnp.where(c == m, NINF, c) for c in cands]
    return m, m0, z


def _route_kernel(x_ref, gm_ref, sc_ref, sh_ref, wqt_ref, sk_ref,
                  ht_ref, s0_ref, s1_ref, e0_ref, e1_ref, thr_ref,
                  qt_scr, a_scr, b_scr, *, tn):
    x = x_ref[...]
    h = _rms(x, gm_ref[...]) * (1.0 + sc_ref[0]) + sh_ref[0]
    ht = h.T.astype(BF)
    ht_ref[...] = ht
    qt_scr[...] = _dot(wqt_ref[...], ht).astype(BF)

    def head(hd, carry):
        st = []
        for p in range(2):
            r0 = pl.multiple_of((hd * 2 + p) * PEER_NKEYS, PEER_NKEYS)
            st.append(_dot(sk_ref[p, hd], qt_scr[pl.ds(r0, PEER_NKEYS), :]))
        for tb in range(tn // LANES):
            lanes = slice(tb * LANES, (tb + 1) * LANES)
            s0 = st[0][:, lanes]
            s1 = st[1][:, lanes]
            s0_ref[tb, hd] = s0
            s1_ref[tb, hd] = s1
            _top16_rows(s0, a_scr)
            _top16_rows(s1, b_scr)
            a = a_scr[...]
            b = b_scr[...]
            thr, _, z = _kth_sum(a, b)
            thr_ref[tb, hd] = thr
            e0_ref[tb, hd] = jnp.exp(s0 - a[0:1]) * (1.0 / z)
            e1_ref[tb, hd] = jnp.exp(s1 - b[0:1])
        return carry

    lax.fori_loop(0, PEER_HEADS, head, 0)


def _route(x, gm, sc, sh, wqt, sk, layer, t_len, tn):
    n = x.shape[0]
    sc_a, sc_s = _mod_operand(sc, t_len, tn)
    sh_a, sh_s = _mod_operand(sh, t_len, tn)
    full = lambda shape: pl.BlockSpec(shape, lambda i: (0,) * len(shape))
    nlb = tn // LANES
    sshape = jax.ShapeDtypeStruct((n // LANES, PEER_HEADS, PEER_NKEYS, LANES), F32)
    sspec = pl.BlockSpec((nlb, PEER_HEADS, PEER_NKEYS, LANES), lambda i: (i, 0, 0, 0))
    return pl.pallas_call(
        functools.partial(_route_kernel, tn=tn),
        grid=(n // tn,),
        in_specs=[
            pl.BlockSpec((tn, D_MODEL), lambda i: (i, 0)),
            full((1, D_MODEL)), sc_s, sh_s,
            pl.BlockSpec((None, 2 * PEER_HEADS * PEER_NKEYS, D_MODEL), lambda i: (layer, 0, 0)),
            pl.BlockSpec((None, 2, PEER_HEADS, PEER_NKEYS, PEER_NKEYS), lambda i: (layer, 0, 0, 0, 0)),
        ],
        out_specs=[pl.BlockSpec((D_MODEL, tn), lambda i: (0, i)), sspec, sspec, sspec, sspec,
                   pl.BlockSpec((nlb, PEER_HEADS, 1, LANES), lambda i: (i, 0, 0, 0))],
        out_shape=[jax.ShapeDtypeStruct((D_MODEL, n), BF), sshape, sshape, sshape, sshape,
                   jax.ShapeDtypeStruct((n // LANES, PEER_HEADS, 1, LANES), F32)],
        scratch_shapes=[pltpu.VMEM((2 * PEER_HEADS * PEER_NKEYS, tn), BF),
                        pltpu.VMEM((PEER_TOPK, LANES), F32), pltpu.VMEM((PEER_TOPK, LANES), F32)],
        compiler_params=_cparams("arbitrary"),
        name="peer_route",
    )(x, gm, sc_a, sh_a, wqt, sk)


PEER_I_PER_STEP = 8


def _expert_kernel(ht_ref, s1_ref, e1_ref, s0_ref, e0_ref, thr_ref, wu_ref, wvt_ref, o_ref,
                   acc_scr, ga_scr, a_scr, *, tn):
    e = pl.program_id(1)

    @pl.when(e == 0)
    def _():
        acc_scr[...] = jnp.zeros_like(acc_scr)

    a_all = _dot(wu_ref[...], ht_ref[...])
    jr = 16
    bcast = lambda r: jnp.broadcast_to(r, (jr, LANES))
    for tb in range(tn // LANES):
        lanes = slice(tb * LANES, (tb + 1) * LANES)
        a_scr[tb] = a_all[:, lanes]

        def jgroup(jg, carry, tb=tb, lanes=lanes):
            j0 = pl.multiple_of(jg * jr, jr)
            g = [None] * PEER_I_PER_STEP
            for hd in range(PEER_HEADS):
                s1 = s1_ref[tb, hd, pl.ds(j0, jr), :]
                e1 = e1_ref[tb, hd, pl.ds(j0, jr), :]
                thr = bcast(thr_ref[tb, hd])
                for il in range(PEER_I_PER_STEP):
                    val = s1 + bcast(s0_ref[tb, hd, il:il + 1, :])
                    w = e1 * bcast(e0_ref[tb, hd, il:il + 1, :])
                    t = jnp.where(val >= thr, w, 0.0)
                    g[il] = t if g[il] is None else g[il] + t
            for il in range(PEER_I_PER_STEP):
                r0 = pl.multiple_of(il * PEER_NKEYS + j0, jr)
                ga_scr[pl.ds(r0, jr), lanes] = (jax.nn.gelu(a_scr[tb, pl.ds(r0, jr), :]) * g[il]).astype(BF)
            return carry

        lax.fori_loop(0, PEER_NKEYS // jr, jgroup, 0)
    acc_scr[...] += _dot(wvt_ref[...], ga_scr[...])

    @pl.when(e == pl.num_programs(1) - 1)
    def _():
        o_ref[...] = acc_scr[...]


def _experts(ht, s0, s1, e0, e1, thr, wu, wvt, layer, tn):
    n = ht.shape[1]
    n_exp = wu.shape[1]
    te = PEER_I_PER_STEP * PEER_NKEYS
    nlb = tn // LANES
    return pl.pallas_call(
        functools.partial(_expert_kernel, tn=tn),
        grid=(n // tn, n_exp // te),
        in_specs=[
            pl.BlockSpec((D_MODEL, tn), lambda t, e: (0, t)),
            pl.BlockSpec((nlb, PEER_HEADS, PEER_NKEYS, LANES), lambda t, e: (t, 0, 0, 0)),
            pl.BlockSpec((nlb, PEER_HEADS, PEER_NKEYS, LANES), lambda t, e: (t, 0, 0, 0)),
            pl.BlockSpec((nlb, PEER_HEADS, PEER_I_PER_STEP, LANES), lambda t, e: (t, 0, e, 0)),
            pl.BlockSpec((nlb, PEER_HEADS, PEER_I_PER_STEP, LANES), lambda t, e: (t, 0, e, 0)),
            pl.BlockSpec((nlb, PEER_HEADS, 1, LANES), lambda t, e: (t, 0, 0, 0)),
            pl.BlockSpec((None, te, D_MODEL), lambda t, e: (layer, e, 0)),
            pl.BlockSpec((None, D_MODEL, te), lambda t, e: (layer, 0, e)),
        ],
        out_specs=pl.BlockSpec((D_MODEL, tn), lambda t, e: (0, t)),
        out_shape=jax.ShapeDtypeStruct((D_MODEL, n), F32),
        scratch_shapes=[pltpu.VMEM((D_MODEL, tn), F32), pltpu.VMEM((te, tn), BF),
                        pltpu.VMEM((nlb, te, LANES), F32)],
        compiler_params=_cparams("arbitrary", "arbitrary"),
        name="peer_experts",
    )(ht, s1, e1, s0, e0, thr, wu, wvt)


def _peer(x, gm, sc, sh, wqt, sk, wu, wvt, layer, t_len):
    n = x.shape[0]
    ht, s0, s1, e0, e1, thr = _route(x, gm, sc, sh, wqt, sk, layer, t_len, 256)
    return _experts(ht, s0, s1, e0, e1, thr, wu, wvt, layer, min(512, n))


def _kvq_kernel(x_ref, pt_ref, g2_ref, kvg_ref, wkv_ref, gm_ref, sc_ref, sh_ref, wq_ref,
                xo_ref, k_ref, v_ref, kb_ref, vb_ref, km_ref, q_ref):
    x = x_ref[...] + g2_ref[0] * pt_ref[...].T
    xo_ref[...] = x
    kv = _dot(_rms(x, kvg_ref[...]).astype(BF), wkv_ref[...])
    k = kv[:, :D_MODEL]
    v = kv[:, D_MODEL:]
    k_ref[...] = k
    v_ref[...] = v
    kb_ref[...] = k.astype(BF)
    vb_ref[...] = v.astype(BF)
    km_ref[0] = jnp.mean(k, axis=0, keepdims=True)
    h = _rms(x, gm_ref[...]) * (1.0 + sc_ref[0]) + sh_ref[0]
    q_ref[...] = _dot(h.astype(BF), wq_ref[...])


def _kvq(x, peer_t, g2, kv_g, w_kv, gm, sc, sh, w_q, t_len):
    n = x.shape[0]
    tm = MOBA_BLOCK
    g2_a, g2_s = _mod_operand(g2, t_len, tm)
    sc_a, sc_s = _mod_operand(sc, t_len, tm)
    sh_a, sh_s = _mod_operand(sh, t_len, tm)
    full = lambda shape: pl.BlockSpec(shape, lambda i: (0,) * len(shape))
    tile = pl.BlockSpec((tm, D_MODEL), lambda i: (i, 0))
    f32 = jax.ShapeDtypeStruct((n, D_MODEL), F32)
    bf = jax.ShapeDtypeStruct((n, D_MODEL), BF)
    return pl.pallas_call(
        _kvq_kernel,
        grid=(n // tm,),
        in_specs=[tile, pl.BlockSpec((D_MODEL, tm), lambda i: (0, i)), g2_s,
                  full((1, D_MODEL)), full((D_MODEL, 2 * D_MODEL)),
                  full((1, D_MODEL)), sc_s, sh_s, full((D_MODEL, D_MODEL))],
        out_specs=[tile, tile, tile, tile, tile,
                   pl.BlockSpec((1, 1, D_MODEL), lambda i: (i, 0, 0)), tile],
        out_shape=[f32, f32, f32, bf, bf, jax.ShapeDtypeStruct((n // tm, 1, D_MODEL), F32), f32],
        compiler_params=_cparams("arbitrary"),
        name="kvq",
    )(x, peer_t, g2_a, kv_g, w_kv, gm, sc_a, sh_a, w_q)


def _third_largest(g):
    m1 = jnp.max(g, axis=1, keepdims=True)
    g2 = jnp.where(g == m1, NINF, g)
    m2 = jnp.max(g2, axis=1, keepdims=True)
    g3 = jnp.where(g2 == m2, NINF, g2)
    return jnp.max(g3, axis=1, keepdims=True)


def _attn_prompt_kernel(c31_ref, q_ref, k_ref, v_ref, km_ref, bo_ref, bp_ref, o_ref, *, nb):
    hp = pl.program_id(1)
    tq = MOBA_BLOCK
    lane = lax.broadcasted_iota(jnp.int32, (tq, LANES), 1)
    row = lax.broadcasted_iota(jnp.int32, (2 * tq, 1), 0)
    c31 = jnp.where(row < tq, c31_ref[2 * hp], c31_ref[2 * hp + 1])
    km = jnp.concatenate([km_ref[0], jnp.zeros((LANES - nb, LANES), F32)], axis=0)
    jidx = lax.broadcasted_iota(jnp.int32, (2 * tq, LANES), 1)

    for i in range(nb):
        q2 = q_ref[i * tq:(i + 1) * tq, :]
        qf = jnp.concatenate([jnp.where(lane < HEAD_DIM, q2, 0.0),
                              jnp.where(lane >= HEAD_DIM, q2, 0.0)], axis=0)
        qs = qf.astype(BF)

        def scores(j):
            return _dot_nt(qs, k_ref[j * tq:(j + 1) * tq, :]) * SCALE

        pieces = []
        if i > MOBA_TOPK:
            gate = lax.dot_general(qf, km, _NT, precision=lax.Precision.HIGHEST,
                                   preferred_element_type=F32)
            gm = jnp.where(jidx < i, gate, NINF)
            selpen = jnp.where(gm >= _third_largest(gm), 0.0, NEG)
            pen = lambda j: selpen[:, j:j + 1]
        else:
            pen = lambda j: 0.0
        for j in range(i - 1):
            pieces.append(scores(j) + (c31 + pen(j)))
        if i >= 1:
            pieces.append(scores(i - 1) + (bp_ref[0] + pen(i - 1)))
        pieces.append(scores(i) + bo_ref[0])

        m = functools.reduce(jnp.maximum, [jnp.max(s, axis=1, keepdims=True) for s in pieces])
        l = None
        acc = None
        for j, s in enumerate(pieces):
            p = jnp.exp(s - m)
            ls = jnp.sum(p, axis=1, keepdims=True)
            o = _dot(p.astype(BF), v_ref[j * tq:(j + 1) * tq, :])
            l = ls if l is None else l + ls
            acc = o if acc is None else acc + o
        out = acc / l
        o_ref[i * tq:(i + 1) * tq, :] = jnp.where(lane < HEAD_DIM, out[:tq], out[tq:]).astype(o_ref.dtype)


def _attn_prompt(q, kb, vb, kmean, bias_own, bias_prev, c31, batch, seq):
    n = q.shape[0]
    nb = seq // MOBA_BLOCK
    tq = MOBA_BLOCK
    grid_spec = pltpu.PrefetchScalarGridSpec(
        num_scalar_prefetch=1,
        grid=(batch, N_HEADS // 2),
        in_specs=[
            pl.BlockSpec((seq, LANES), lambda b, hp, c: (b, hp)),
            pl.BlockSpec((seq, LANES), lambda b, hp, c: (b, hp)),
            pl.BlockSpec((seq, LANES), lambda b, hp, c: (b, hp)),
            pl.BlockSpec((1, nb, LANES), lambda b, hp, c: (b, 0, hp)),
            pl.BlockSpec((1, 2 * tq, tq), lambda b, hp, c: (hp, 0, 0)),
            pl.BlockSpec((1, 2 * tq, tq), lambda b, hp, c: (hp, 0, 0)),
        ],
        out_specs=pl.BlockSpec((seq, LANES), lambda b, hp, c: (b, hp)),
    )
    return pl.pallas_call(
        functools.partial(_attn_prompt_kernel, nb=nb),
        grid_spec=grid_spec,
        out_shape=jax.ShapeDtypeStruct((n, D_MODEL), BF),
        compiler_params=_cparams("arbitrary", "arbitrary"),
        name="attn_prompt",
    )(c31, q, kb, vb, kmean.reshape(batch, nb, D_MODEL), bias_own, bias_prev)


SAMPLE_PAGES_PER_STEP = 4


def _attn_sample_kernel(pt_ref, q_ref, kn_ref, vn_ref, *refs, n_pages, ppb, n_near, ds):
    npg = SAMPLE_PAGES_PER_STEP
    k_refs = refs[:npg]
    v_refs = refs[npg:2 * npg]
    bfar_ref, bnear_ref, bown_ref, o_ref, o_scr, m_scr, l_scr, g_scr = refs[2 * npg:]
    step = pl.program_id(1)
    last = step == pl.num_programs(1) - 1
    rows = N_HEADS * ds
    wide = lambda c: jnp.broadcast_to(c, (rows, HEAD_DIM))
    qf = q_ref[0]
    qb = (qf * SCALE).astype(BF)
    row_head = lax.broadcasted_iota(jnp.int32, (rows, LANES), 0) // ds
    lane = lax.broadcasted_iota(jnp.int32, (rows, LANES), 1)
    far = jnp.concatenate([bfar_ref[...]] * (PAGE_SIZE * N_HEADS // LANES), axis=1)
    zpad = jnp.zeros((LANES - N_HEADS, HEAD_DIM), F32)

    gate = None
    for pg in range(npg):
        page = step * npg + pg
        kf = k_refs[pg][0]
        s = _dot_nt(qb, kf.astype(BF))
        if pg >= npg - n_near:
            s = s + jnp.where(last, bnear_ref[pg - (npg - n_near)], far)
        else:
            s = s + far
        m = jnp.max(s, axis=1, keepdims=True)
        p = jnp.exp(s - m)
        o_scr[page] = _dot(p.astype(BF), v_refs[pg][0].astype(BF))
        m_scr[page] = wide(m)
        l_scr[page] = wide(jnp.sum(p, axis=1, keepdims=True))
        ksum = jnp.sum(kf.reshape(PAGE_SIZE, N_HEADS, HEAD_DIM), axis=0)
        gall = lax.dot_general(qf, jnp.concatenate([ksum, zpad], axis=0), _NT,
                               precision=lax.Precision.HIGHEST, preferred_element_type=F32)
        gpage = jnp.sum(jnp.where(lane == row_head, gall, 0.0), axis=1, keepdims=True)
        gate = gpage if pg % ppb == 0 else gate + gpage
        if pg % ppb == ppb - 1:
            g_scr[step * (npg // ppb) + pg // ppb] = wide(gate * (1.0 / MOBA_BLOCK))

    @pl.when(last)
    def _():
        so = _dot_nt(qb, kn_ref[0].astype(BF)) + bown_ref[...]
        mo = jnp.max(so, axis=1, keepdims=True)
        po = jnp.exp(so - mo)
        lo = wide(jnp.sum(po, axis=1, keepdims=True))
        mo = wide(mo)
        oo = _dot(po.astype(BF), vn_ref[0].astype(BF))

        gs = [g_scr[b] for b in range(n_pages // ppb)]
        g1 = functools.reduce(jnp.maximum, gs)
        gs2 = [jnp.where(g == g1, NINF, g) for g in gs]
        g2 = functools.reduce(jnp.maximum, gs2)
        gs3 = [jnp.where(g == g2, NINF, g) for g in gs2]
        g3 = functools.reduce(jnp.maximum, gs3)
        sel = [g >= g3 for g in gs]

        mm = mo
        for pgi in range(n_pages):
            mm = jnp.maximum(mm, jnp.where(sel[pgi // ppb], m_scr[pgi], NINF))
        wo = jnp.exp(mo - mm)
        lsum = lo * wo
        acc = oo * wo
        for pgi in range(n_pages):
            wp = jnp.where(sel[pgi // ppb], jnp.exp(m_scr[pgi] - mm), 0.0)
            lsum = lsum + l_scr[pgi] * wp
            acc = acc + o_scr[pgi] * wp
        o_ref[0] = acc / lsum


def _attn_sample(q, kn, vn, cache_k, cache_v, page_table, bias_far, bias_near, bias_own):
    db, rows, _ = q.shape
    ds = rows // N_HEADS
    assert ds * N_HEADS == LANES, "own-block tables assume DS * heads == 128 lanes"
    n_pages = page_table.shape[1]
    ppb = MOBA_BLOCK // PAGE_SIZE
    npg = SAMPLE_PAGES_PER_STEP
    n_near = bias_near.shape[0]
    assert n_pages % npg == 0 and npg % ppb == 0 and n_near <= npg
    n_pool = cache_k.shape[0]
    prow = PAGE_SIZE * N_HEADS
    ck = cache_k.reshape(n_pool, prow, HEAD_DIM)
    cv = cache_v.reshape(n_pool, prow, HEAD_DIM)
    pt = page_table.reshape(-1)

    def page_spec(k):
        return pl.BlockSpec((1, prow, HEAD_DIM),
                            lambda s, j, pt_ref: (pt_ref[s * n_pages + j * npg + k], 0, 0))

    seq_spec = pl.BlockSpec((1, rows, HEAD_DIM), lambda s, j, pt_ref: (s, 0, 0))
    pages = [page_spec(k) for k in range(npg)]
    stat = pltpu.VMEM((n_pages, rows, HEAD_DIM), F32)
    grid_spec = pltpu.PrefetchScalarGridSpec(
        num_scalar_prefetch=1,
        grid=(db, n_pages // npg),
        in_specs=[seq_spec, seq_spec, seq_spec] + pages + pages + [
            pl.BlockSpec((rows, LANES), lambda s, j, pt_ref: (0, 0)),
            pl.BlockSpec((n_near, rows, prow), lambda s, j, pt_ref: (0, 0, 0)),
            pl.BlockSpec((rows, LANES), lambda s, j, pt_ref: (0, 0))],
        out_specs=seq_spec,
        scratch_shapes=[stat, stat, stat, pltpu.VMEM((n_pages // ppb, rows, HEAD_DIM), F32)],
    )
    return pl.pallas_call(
        functools.partial(_attn_sample_kernel, n_pages=n_pages, ppb=ppb, n_near=n_near, ds=ds),
        grid_spec=grid_spec,
        out_shape=jax.ShapeDtypeStruct((db, rows, HEAD_DIM), F32),
        compiler_params=_cparams("arbitrary", "arbitrary"),
        name="attn_sample",
    )(pt, q, kn, vn, *([ck] * npg), *([cv] * npg), bias_far, bias_near, bias_own)


def _proj_res_kernel(x_ref, a_ref, g_ref, w_ref, o_ref):
    o_ref[...] = x_ref[...] + g_ref[0] * _dot(a_ref[...].astype(BF), w_ref[...])


def _proj_res(x, a, g, w, t_len, tm):
    n = x.shape[0]
    g_a, g_s = _mod_operand(g, t_len, tm)
    tile = pl.BlockSpec((tm, D_MODEL), lambda i: (i, 0))
    return pl.pallas_call(
        _proj_res_kernel,
        grid=(n // tm,),
        in_specs=[tile, tile, g_s, pl.BlockSpec((D_MODEL, D_MODEL), lambda i: (0, 0))],
        out_specs=tile,
        out_shape=jax.ShapeDtypeStruct((n, D_MODEL), F32),
        compiler_params=_cparams("arbitrary"),
        name="proj_res",
    )(x, a, g_a, w)


def _final_kernel(x_ref, pt_ref, g2_ref, fg_ref, o_ref):
    x = x_ref[...] + g2_ref[0] * pt_ref[...].T
    o_ref[...] = _rms(x, fg_ref[...])


def _final(x, peer_t, g2, fg, t_len, tm):
    n = x.shape[0]
    g_a, g_s = _mod_operand(g2, t_len, tm)
    tile = pl.BlockSpec((tm, D_MODEL), lambda i: (i, 0))
    return pl.pallas_call(
        _final_kernel,
        grid=(n // tm,),
        in_specs=[tile, pl.BlockSpec((D_MODEL, tm), lambda i: (0, i)), g_s,
                  pl.BlockSpec((1, D_MODEL), lambda i: (0, 0))],
        out_specs=tile,
        out_shape=jax.ShapeDtypeStruct((n, D_MODEL), F32),
        compiler_params=_cparams("arbitrary"),
        name="final_norm",
    )(x, peer_t, g_a, fg)


def _transpose_kernel(x_ref, o_ref):
    o_ref[0] = x_ref[0].T.astype(BF)


def _transpose_bf16(x, tr=512):
    nl, r, c = x.shape
    return pl.pallas_call(
        _transpose_kernel,
        grid=(nl, r // tr),
        in_specs=[pl.BlockSpec((1, tr, c), lambda l, i: (l, i, 0))],
        out_specs=pl.BlockSpec((1, c, tr), lambda l, i: (l, 0, i)),
        out_shape=jax.ShapeDtypeStruct((nl, c, r), BF),
        compiler_params=_cparams("arbitrary", "arbitrary"),
        name="transpose_bf16",
    )(x)


def _t5_bucket(rel):
    n = jnp.maximum(rel, 0)
    max_exact = N_BUCKETS // 2
    nf = jnp.maximum(n, max_exact).astype(F32)
    large = max_exact + (jnp.log(nf / max_exact) / math.log(MAX_DISTANCE / max_exact)
                         * (N_BUCKETS - max_exact)).astype(jnp.int32)
    large = jnp.minimum(large, N_BUCKETS - 1)
    return jnp.where(n < max_exact, n, large)


def _bias_table(rel, rel_bias, causal):
    onehot = (_t5_bucket(rel)[..., None] == jnp.arange(N_BUCKETS, dtype=jnp.int32)).astype(F32)
    b = jnp.einsum("rcb,bh->hrc", onehot, rel_bias, precision=lax.Precision.HIGHEST)
    if causal:
        b = jnp.where((rel >= 0)[None], b, NEG)
    return b


def _prompt_bias(rel_bias):
    t = jnp.arange(MOBA_BLOCK, dtype=jnp.int32)
    d = t[:, None] - t[None, :]
    own = _bias_table(d, rel_bias, True)
    prev = _bias_table(d + MOBA_BLOCK, rel_bias, False)
    pair = lambda b: b.reshape(N_HEADS // 2, 2 * MOBA_BLOCK, MOBA_BLOCK)
    far = rel_bias[_t5_bucket(jnp.int32(MOBA_BLOCK + 1))]
    return pair(own), pair(prev), far


def _t5_saturation():
    r = N_BUCKETS // 2
    while N_BUCKETS // 2 + int(math.log(r / (N_BUCKETS // 2)) / math.log(MAX_DISTANCE / (N_BUCKETS // 2))
                               * (N_BUCKETS - N_BUCKETS // 2)) < N_BUCKETS - 1:
        r += 1
    return r + 1


def _sample_near_pages(ds, past):
    n_pages = past // PAGE_SIZE
    sat = _t5_saturation()
    return sum(1 for p in range(n_pages) if past - ((p + 1) * PAGE_SIZE - 1) < sat)


def _sample_bias(rel_bias, ds, past):
    rows = N_HEADS * ds
    t = jnp.arange(ds, dtype=jnp.int32)
    l = jnp.arange(PAGE_SIZE, dtype=jnp.int32)
    hh = jnp.arange(N_HEADS)
    same = hh[:, None] == hh[None, :]

    def expand(tab):
        full = jnp.where(same[:, None, None, :], tab[:, :, :, None], NEG)
        return full.reshape(rows, tab.shape[2] * N_HEADS)

    n_pages = past // PAGE_SIZE
    n_near = _sample_near_pages(ds, past)
    far_val = rel_bias[_t5_bucket(jnp.int32(_t5_saturation()))]
    far = expand(jnp.broadcast_to(far_val[:, None, None], (N_HEADS, ds, LANES // N_HEADS)))
    near = jnp.stack([
        expand(_bias_table(past + t[:, None] - (p * PAGE_SIZE + l[None, :]), rel_bias, False))
        for p in range(n_pages - n_near, n_pages)])
    own = expand(_bias_table(t[:, None] - t[None, :], rel_bias, True))
    return far, near, own


def _gate_weights(w_s, b_s, chunk_len):
    w = jnp.where(jnp.tril(jnp.ones((chunk_len, chunk_len), bool)), w_s[:, :chunk_len, :chunk_len], 0)
    reps = CHUNK // chunk_len
    if reps > 1:
        eye = jnp.eye(reps, dtype=w.dtype)
        w = jnp.einsum("ab,gts->gatbs", eye, w).reshape(A_GROUPS, CHUNK, CHUNK)
    bias = jnp.tile(b_s[:, :chunk_len].T, (reps, 1))
    bias = jnp.repeat(bias, D_MODEL // A_GROUPS, axis=1)
    return w.astype(BF), bias


def kernel(x_prompt, x_sample, cache_k, cache_v, page_table, c_prompt, c_sample, a_w_in, a_g_v, a_w_s, a_b_s, a_w_out, b_w_q, b_w_o, kv_g, w_kv, rel_bias, g_mix, g_ffn, w_ada, b_ada, p_w_q, p_sub_keys, p_w_u, p_w_v, final_g):
    batch, seq, _ = x_prompt.shape
    db, ds, _ = x_sample.shape
    n_p = batch * seq
    n_s = db * ds
    past = page_table.shape[1] * PAGE_SIZE
    row = lambda v: v.reshape(1, D_MODEL)

    w_in = a_w_in[0].astype(BF)
    w_out = a_w_out[0].astype(BF)
    wkv = w_kv.astype(BF)
    wq_attn = b_w_q[0].astype(BF)
    wo_attn = b_w_o[0].astype(BF)
    wqt = _transpose_bf16(p_w_q)
    sk = p_sub_keys.astype(BF)
    wu = p_w_u.astype(BF)
    wvt = _transpose_bf16(p_w_v)

    mod = _ada(jnp.concatenate([c_prompt, c_sample], axis=0), w_ada, b_ada)

    def mods(l, lo, hi):
        m = mod[l, lo:hi]
        return [m[:, k * D_MODEL:(k + 1) * D_MODEL] for k in range(6)]

    bias_own_p, bias_prev_p, c31 = _prompt_bias(rel_bias)
    bias_far_s, bias_near_s, bias_own_s = _sample_bias(rel_bias, ds, past)

    def trunk(x, t_len, lo, hi, gate_w, gate_b, tm, attend):
        sh1, sc1, g1, sh2, sc2, g2 = mods(0, lo, hi)
        x1, v_rows = _gmlp(x, row(g_mix[0]), sc1, sh1, g1, w_in, row(a_g_v[0]), gate_w, gate_b, w_out,
                           t_len, tm)
        peer_t = _peer(x1, row(g_ffn[0]), sc2, sh2, wqt, sk, wu, wvt, 0, t_len)
        sh1b, sc1b, g1b, sh2b, sc2b, g2b = mods(1, lo, hi)
        x2, k, v, kb, vb, kmean, q = _kvq(x1, peer_t, g2, row(kv_g), wkv, row(g_mix[1]), sc1b, sh1b,
                                          wq_attn, t_len)
        att = attend(q, k, v, kb, vb, kmean)
        x3 = _proj_res(x2, att, g1b, wo_attn, t_len, MOBA_BLOCK)
        peer_t = _peer(x3, row(g_ffn[1]), sc2b, sh2b, wqt, sk, wu, wvt, 1, t_len)
        y = _final(x3, peer_t, g2b, row(final_g), t_len, MOBA_BLOCK)
        return y, k, v, v_rows

    def attend_prompt(q, k, v, kb, vb, kmean):
        return _attn_prompt(q, kb, vb, kmean, bias_own_p, bias_prev_p, c31, batch, seq)

    def attend_sample(q, k, v, kb, vb, kmean):
        q_ht = q.reshape(db, ds, N_HEADS, HEAD_DIM).transpose(0, 2, 1, 3).reshape(db, N_HEADS * ds, HEAD_DIM)
        out = _attn_sample(q_ht, k.reshape(db, ds * N_HEADS, HEAD_DIM), v.reshape(db, ds * N_HEADS, HEAD_DIM),
                           cache_k, cache_v, page_table, bias_far_s, bias_near_s, bias_own_s)
        out = out.reshape(db, N_HEADS, ds, HEAD_DIM).transpose(0, 2, 1, 3)
        return out.reshape(n_s, D_MODEL)

    gw_p, gb_p = _gate_weights(a_w_s[0], a_b_s[0], min(seq, CHUNK))
    gw_s, gb_s = _gate_weights(a_w_s[0], a_b_s[0], min(ds, CHUNK))

    y_p, k_p, v_p, _ = trunk(x_prompt.reshape(n_p, D_MODEL), seq, 0, batch, gw_p, gb_p, 512, attend_prompt)
    y_s, k_s, v_s, cv_s = trunk(x_sample.reshape(n_s, D_MODEL), ds, batch, batch + db, gw_s, gb_s, 128,
                                attend_sample)

    heads = lambda a, b, t: a.reshape(b, t, N_HEADS, HEAD_DIM)
    return (y_p.reshape(batch, seq, D_MODEL), y_s.reshape(db, ds, D_MODEL),
            heads(k_p, batch, seq), heads(v_p, batch, seq),
            heads(k_s, db, ds), heads(v_s, db, ds),
            cv_s.reshape(1, db, ds, D_MODEL))
```

```python
import functools
import math

import jax
import jax.numpy as jnp
from jax import lax
from jax.experimental import pallas as pl
from jax.experimental.pallas import tpu as pltpu

D_MODEL = 1024
N_HEADS = 16
HEAD_DIM = 64
CHUNK = 128
A_GROUPS = 8
MOBA_BLOCK = 256
MOBA_TOPK = 3
PAGE_SIZE = 128
N_BUCKETS = 32
MAX_DISTANCE = 128
PEER_HEADS = 8
PEER_NKEYS = 128
PEER_TOPK = 16
EPS = 1e-6
NEG = -1e30
NINF = float("-inf")
SCALE = HEAD_DIM ** -0.5

BF = jnp.bfloat16
F32 = jnp.float32

LANES = 128
VMEM_LIMIT = 56 * 1024 * 1024

_NT = (((1,), (1,)), ((), ()))


def _cparams(*sem):
    return pltpu.CompilerParams(dimension_semantics=sem, vmem_limit_bytes=VMEM_LIMIT)


def _rms(x, g):
    ms = jnp.mean(x * x, axis=-1, keepdims=True)
    return x * lax.rsqrt(ms + EPS) * g


def _dot(a, b):
    return jnp.dot(a, b, preferred_element_type=F32)


def _dot_nt(a, b):
    return lax.dot_general(a, b, _NT, preferred_element_type=F32)


def _ada_kernel(c_ref, w_ref, b_ref, o_ref):
    c = c_ref[...]
    s = (c * jax.nn.sigmoid(c)).astype(BF)
    o_ref[0] = _dot(s, w_ref[0].astype(BF)) + b_ref[0]


def _ada(c_all, w_ada, b_ada):
    m = c_all.shape[0]
    depth, _, n6 = w_ada.shape
    tn = 1536
    return pl.pallas_call(
        _ada_kernel,
        grid=(depth, n6 // tn),
        in_specs=[
            pl.BlockSpec((m, D_MODEL), lambda l, j: (0, 0)),
            pl.BlockSpec((1, D_MODEL, tn), lambda l, j: (l, 0, j)),
            pl.BlockSpec((1, 1, tn), lambda l, j: (l, 0, j)),
        ],
        out_specs=pl.BlockSpec((1, m, tn), lambda l, j: (l, 0, j)),
        out_shape=jax.ShapeDtypeStruct((depth, m, n6), F32),
        compiler_params=_cparams("arbitrary", "arbitrary"),
        name="ada",
    )(c_all, w_ada, b_ada.reshape(depth, 1, n6))


def _mod_operand(arr, t_len, tm):
    b = arr.shape[0]
    if t_len % tm == 0:
        per = t_len // tm
        return arr.reshape(b, 1, D_MODEL), pl.BlockSpec((1, 1, D_MODEL), lambda i, *_: (i // per, 0, 0))
    n = b * t_len
    rep = jnp.repeat(arr, t_len, axis=0).reshape(n // tm, tm, D_MODEL)
    return rep, pl.BlockSpec((1, tm, D_MODEL), lambda i, *_: (i, 0, 0))


def _gmlp_kernel(x_ref, gm_ref, sc_ref, sh_ref, g1_ref, win_ref, gv_ref, wg_ref, bs_ref, wout_ref,
                 xo_ref, v_ref, us_scr, *, tm):
    x = x_ref[...]
    h = _rms(x, gm_ref[...]) * (1.0 + sc_ref[0]) + sh_ref[0]
    uv = jax.nn.gelu(_dot(h.astype(BF), win_ref[...]))
    u = uv[:, :D_MODEL]
    v = _rms(uv[:, D_MODEL:], gv_ref[...])
    v_ref[...] = v
    vb = v.astype(BF)
    for c in range(tm // CHUNK):
        r0 = c * CHUNK
        parts = []
        for g in range(A_GROUPS):
            c0 = g * CHUNK
            parts.append(_dot(wg_ref[g], vb[r0:r0 + CHUNK, c0:c0 + CHUNK]))
        s = jnp.concatenate(parts, axis=1) + bs_ref[...]
        us_scr[r0:r0 + CHUNK, :] = (u[r0:r0 + CHUNK, :] * s).astype(BF)
    out = _dot(us_scr[...], wout_ref[...])
    xo_ref[...] = x + g1_ref[0] * out


def _gmlp(x, gm, sc, sh, g1, w_in, g_v, wg, bs, w_out, t_len, tm):
    n = x.shape[0]
    sc_a, sc_s = _mod_operand(sc, t_len, tm)
    sh_a, sh_s = _mod_operand(sh, t_len, tm)
    g1_a, g1_s = _mod_operand(g1, t_len, tm)
    full = lambda shape: pl.BlockSpec(shape, lambda i: (0,) * len(shape))
    return pl.pallas_call(
        functools.partial(_gmlp_kernel, tm=tm),
        grid=(n // tm,),
        in_specs=[
            pl.BlockSpec((tm, D_MODEL), lambda i: (i, 0)),
            full((1, D_MODEL)), sc_s, sh_s, g1_s,
            full((D_MODEL, 2 * D_MODEL)), full((1, D_MODEL)),
            full((A_GROUPS, CHUNK, CHUNK)), full((CHUNK, D_MODEL)),
            full((D_MODEL, D_MODEL)),
        ],
        out_specs=[pl.BlockSpec((tm, D_MODEL), lambda i: (i, 0)),
                   pl.BlockSpec((tm, D_MODEL), lambda i: (i, 0))],
        out_shape=[jax.ShapeDtypeStruct((n, D_MODEL), F32), jax.ShapeDtypeStruct((n, D_MODEL), F32)],
        scratch_shapes=[pltpu.VMEM((tm, D_MODEL), BF)],
        compiler_params=_cparams("arbitrary"),
        name="gmlp",
    )(x, gm, sc_a, sh_a, g1_a, w_in, g_v, wg, bs, w_out)


SUBLANES = 8


def _oddeven_sort_pairs(n):
    pairs = []
    p = 1
    while p < n:
        k = p
        while k >= 1:
            for j in range(k % p, n - k, 2 * k):
                for i in range(min(k, n - j - k)):
                    if (i + j) // (2 * p) == (i + j + k) // (2 * p):
                        pairs.append((i + j, i + j + k))
            k //= 2
        p *= 2
    return pairs


_SORT16 = _oddeven_sort_pairs(PEER_TOPK)
_BITONIC16 = [(i, i | k) for k in (8, 4, 2, 1) for i in range(PEER_TOPK) if not i & k]


def _cex(v, pairs):
    for i, j in pairs:
        v[i], v[j] = jnp.maximum(v[i], v[j]), jnp.minimum(v[i], v[j])


def _top16_sorted(v):
    v = list(v)
    _cex(v, _SORT16)
    for shift in (4, 2, 1):
        w = [pltpu.roll(x, shift, axis=0) for x in v]
        v = [jnp.maximum(v[k], w[PEER_TOPK - 1 - k]) for k in range(PEER_TOPK)]
        _cex(v, _BITONIC16)
    return v


def _kth_sum(a, b):
    shape = a[0].shape
    row = lax.broadcasted_iota(jnp.int32, shape, 0)

    def stack(xs):
        out = xs[SUBLANES - 1]
        for r in range(SUBLANES - 2, -1, -1):
            out = jnp.where(row == r, xs[r], out)
        return out

    a_lo, a_hi, b_hi = stack(a[:SUBLANES]), stack(a[SUBLANES:]), stack(b[SUBLANES:])
    cands = [a_lo + b[0], a_hi + b[0]]
    for l in range(1, SUBLANES):
        kmax = PEER_TOPK // (l + 1)
        c = a_lo + b[l]
        if kmax < SUBLANES:
            c = jnp.where(row < kmax, c, NINF)
        cands.append(c)
    cands.append(b_hi + a[0])
    cands += [jnp.full(shape, NINF, F32)] * (PEER_TOPK - len(cands))
    return _top16_sorted(cands)


def _route_kernel(x_ref, gm_ref, sc_ref, sh_ref, wqt_ref, sk_ref,
                  ht_ref, s0_ref, s1_ref, e0_ref, e1_ref, thr_ref,
                  qt_scr, *, tn):
    x = x_ref[...]
    h = _rms(x, gm_ref[...]) * (1.0 + sc_ref[0]) + sh_ref[0]
    ht = h.T.astype(BF)
    ht_ref[...] = ht
    qt_scr[...] = _dot(wqt_ref[...], ht).astype(BF)

    def head(hd, carry):
        st = []
        for p in range(2):
            r0 = pl.multiple_of((hd * 2 + p) * PEER_NKEYS, PEER_NKEYS)
            st.append(_dot(sk_ref[p, hd], qt_scr[pl.ds(r0, PEER_NKEYS), :]))
        for tb in range(tn // LANES):
            lanes = slice(tb * LANES, (tb + 1) * LANES)
            s0 = st[0][:, lanes]
            s1 = st[1][:, lanes]
            s0_ref[tb, hd] = s0
            s1_ref[tb, hd] = s1
            tiles = lambda s: [s[SUBLANES * r:SUBLANES * (r + 1)] for r in range(PEER_NKEYS // SUBLANES)]
            a = _top16_sorted(tiles(s0))
            b = _top16_sorted(tiles(s1))
            top = _kth_sum(a, b)
            z = functools.reduce(jnp.add, [jnp.exp(t - top[0]) for t in top[1:]]) + 1.0
            thr_ref[tb, hd] = top[PEER_TOPK - 1][0:1]
            e0_ref[tb, hd] = jnp.exp(s0 - a[0][0:1]) * (1.0 / z[0:1])
            e1_ref[tb, hd] = jnp.exp(s1 - b[0][0:1])
        return carry

    lax.fori_loop(0, PEER_HEADS, head, 0)


def _route(x, gm, sc, sh, wqt, sk, layer, t_len, tn):
    n = x.shape[0]
    sc_a, sc_s = _mod_operand(sc, t_len, tn)
    sh_a, sh_s = _mod_operand(sh, t_len, tn)
    full = lambda shape: pl.BlockSpec(shape, lambda i: (0,) * len(shape))
    nlb = tn // LANES
    sshape = jax.ShapeDtypeStruct((n // LANES, PEER_HEADS, PEER_NKEYS, LANES), F32)
    sspec = pl.BlockSpec((nlb, PEER_HEADS, PEER_NKEYS, LANES), lambda i: (i, 0, 0, 0))
    return pl.pallas_call(
        functools.partial(_route_kernel, tn=tn),
        grid=(n // tn,),
        in_specs=[
            pl.BlockSpec((tn, D_MODEL), lambda i: (i, 0)),
            full((1, D_MODEL)), sc_s, sh_s,
            pl.BlockSpec((None, 2 * PEER_HEADS * PEER_NKEYS, D_MODEL), lambda i: (layer, 0, 0)),
            pl.BlockSpec((None, 2, PEER_HEADS, PEER_NKEYS, PEER_NKEYS), lambda i: (layer, 0, 0, 0, 0)),
        ],
        out_specs=[pl.BlockSpec((D_MODEL, tn), lambda i: (0, i)), sspec, sspec, sspec, sspec,
                   pl.BlockSpec((nlb, PEER_HEADS, 1, LANES), lambda i: (i, 0, 0, 0))],
        out_shape=[jax.ShapeDtypeStruct((D_MODEL, n), BF), sshape, sshape, sshape, sshape,
                   jax.ShapeDtypeStruct((n // LANES, PEER_HEADS, 1, LANES), F32)],
        scratch_shapes=[pltpu.VMEM((2 * PEER_HEADS * PEER_NKEYS, tn), BF)],
        compiler_params=_cparams("arbitrary"),
        name="peer_route",
    )(x, gm, sc_a, sh_a, wqt, sk)


PEER_I_PER_STEP = 8


def _expert_kernel(ht_ref, s1_ref, e1_ref, s0_ref, e0_ref, thr_ref, wu_ref, wvt_ref, o_ref,
                   acc_scr, ga_scr, a_scr, *, tn):
    e = pl.program_id(1)

    @pl.when(e == 0)
    def _():
        acc_scr[...] = jnp.zeros_like(acc_scr)

    a_scr[...] = _dot(wu_ref[...], ht_ref[...])
    jr = 16
    bcast = lambda r: jnp.broadcast_to(r, (jr, LANES))
    for tb in range(tn // LANES):
        lanes = slice(tb * LANES, (tb + 1) * LANES)

        def jgroup(jg, carry, tb=tb, lanes=lanes):
            j0 = pl.multiple_of(jg * jr, jr)
            g = [None] * PEER_I_PER_STEP
            for hd in range(PEER_HEADS):
                s1 = s1_ref[tb, hd, pl.ds(j0, jr), :]
                e1 = e1_ref[tb, hd, pl.ds(j0, jr), :]
                thr = bcast(thr_ref[tb, hd])
                for il in range(PEER_I_PER_STEP):
                    val = s1 + bcast(s0_ref[tb, hd, il:il + 1, :])
                    w = e1 * bcast(e0_ref[tb, hd, il:il + 1, :])
                    t = jnp.where(val >= thr, w, 0.0)
                    g[il] = t if g[il] is None else g[il] + t
            for il in range(PEER_I_PER_STEP):
                r0 = pl.multiple_of(il * PEER_NKEYS + j0, jr)
                ga_scr[pl.ds(r0, jr), lanes] = (jax.nn.gelu(a_scr[pl.ds(r0, jr), lanes]) * g[il]).astype(BF)
            return carry

        lax.fori_loop(0, PEER_NKEYS // jr, jgroup, 0)
    acc_scr[...] += _dot(wvt_ref[...], ga_scr[...])

    @pl.when(e == pl.num_programs(1) - 1)
    def _():
        o_ref[...] = acc_scr[...]


def _experts(ht, s0, s1, e0, e1, thr, wu, wvt, layer, tn):
    n = ht.shape[1]
    n_exp = wu.shape[1]
    te = PEER_I_PER_STEP * PEER_NKEYS
    nlb = tn // LANES
    return pl.pallas_call(
        functools.partial(_expert_kernel, tn=tn),
        grid=(n // tn, n_exp // te),
        in_specs=[
            pl.BlockSpec((D_MODEL, tn), lambda t, e: (0, t)),
            pl.BlockSpec((nlb, PEER_HEADS, PEER_NKEYS, LANES), lambda t, e: (t, 0, 0, 0)),
            pl.BlockSpec((nlb, PEER_HEADS, PEER_NKEYS, LANES), lambda t, e: (t, 0, 0, 0)),
            pl.BlockSpec((nlb, PEER_HEADS, PEER_I_PER_STEP, LANES), lambda t, e: (t, 0, e, 0)),
            pl.BlockSpec((nlb, PEER_HEADS, PEER_I_PER_STEP, LANES), lambda t, e: (t, 0, e, 0)),
            pl.BlockSpec((nlb, PEER_HEADS, 1, LANES), lambda t, e: (t, 0, 0, 0)),
            pl.BlockSpec((None, te, D_MODEL), lambda t, e: (layer, e, 0)),
            pl.BlockSpec((None, D_MODEL, te), lambda t, e: (layer, 0, e)),
        ],
        out_specs=pl.BlockSpec((D_MODEL, tn), lambda t, e: (0, t)),
        out_shape=jax.ShapeDtypeStruct((D_MODEL, n), F32),
        scratch_shapes=[pltpu.VMEM((D_MODEL, tn), F32), pltpu.VMEM((te, tn), BF),
                        pltpu.VMEM((te, tn), F32)],
        compiler_params=_cparams("arbitrary", "arbitrary"),
        name="peer_experts",
    )(ht, s1, e1, s0, e0, thr, wu, wvt)


def _peer(x, gm, sc, sh, wqt, sk, wu, wvt, layer, t_len):
    n = x.shape[0]
    ht, s0, s1, e0, e1, thr = _route(x, gm, sc, sh, wqt, sk, layer, t_len, 256)
    return _experts(ht, s0, s1, e0, e1, thr, wu, wvt, layer, min(512, n))


def _kvq_kernel(x_ref, pt_ref, g2_ref, kvg_ref, wkv_ref, gm_ref, sc_ref, sh_ref, wq_ref,
                xo_ref, k_ref, v_ref, kb_ref, vb_ref, km_ref, q_ref):
    x = x_ref[...] + g2_ref[0] * pt_ref[...].T
    xo_ref[...] = x
    kv = _dot(_rms(x, kvg_ref[...]).astype(BF), wkv_ref[...])
    k = kv[:, :D_MODEL]
    v = kv[:, D_MODEL:]
    k_ref[...] = k
    v_ref[...] = v
    kb_ref[...] = k.astype(BF)
    vb_ref[...] = v.astype(BF)
    km_ref[0] = jnp.mean(k, axis=0, keepdims=True)
    h = _rms(x, gm_ref[...]) * (1.0 + sc_ref[0]) + sh_ref[0]
    q_ref[...] = _dot(h.astype(BF), wq_ref[...])


def _kvq(x, peer_t, g2, kv_g, w_kv, gm, sc, sh, w_q, t_len):
    n = x.shape[0]
    tm = MOBA_BLOCK
    g2_a, g2_s = _mod_operand(g2, t_len, tm)
    sc_a, sc_s = _mod_operand(sc, t_len, tm)
    sh_a, sh_s = _mod_operand(sh, t_len, tm)
    full = lambda shape: pl.BlockSpec(shape, lambda i: (0,) * len(shape))
    tile = pl.BlockSpec((tm, D_MODEL), lambda i: (i, 0))
    f32 = jax.ShapeDtypeStruct((n, D_MODEL), F32)
    bf = jax.ShapeDtypeStruct((n, D_MODEL), BF)
    return pl.pallas_call(
        _kvq_kernel,
        grid=(n // tm,),
        in_specs=[tile, pl.BlockSpec((D_MODEL, tm), lambda i: (0, i)), g2_s,
                  full((1, D_MODEL)), full((D_MODEL, 2 * D_MODEL)),
                  full((1, D_MODEL)), sc_s, sh_s, full((D_MODEL, D_MODEL))],
        out_specs=[tile, tile, tile, tile, tile,
                   pl.BlockSpec((1, 1, D_MODEL), lambda i: (i, 0, 0)), tile],
        out_shape=[f32, f32, f32, bf, bf, jax.ShapeDtypeStruct((n // tm, 1, D_MODEL), F32), f32],
        compiler_params=_cparams("arbitrary"),
        name="kvq",
    )(x, peer_t, g2_a, kv_g, w_kv, gm, sc_a, sh_a, w_q)


def _third_largest(g):
    m1 = jnp.max(g, axis=1, keepdims=True)
    g2 = jnp.where(g == m1, NINF, g)
    m2 = jnp.max(g2, axis=1, keepdims=True)
    g3 = jnp.where(g2 == m2, NINF, g2)
    return jnp.max(g3, axis=1, keepdims=True)


def _attn_prompt_kernel(c31_ref, q_ref, k_ref, v_ref, km_ref, bo_ref, bp_ref, o_ref, *, nb):
    hp = pl.program_id(1)
    tq = MOBA_BLOCK
    lane = lax.broadcasted_iota(jnp.int32, (tq, LANES), 1)
    row = lax.broadcasted_iota(jnp.int32, (2 * tq, 1), 0)
    c31 = jnp.where(row < tq, c31_ref[2 * hp], c31_ref[2 * hp + 1])
    km = jnp.concatenate([km_ref[0], jnp.zeros((LANES - nb, LANES), F32)], axis=0)
    jidx = lax.broadcasted_iota(jnp.int32, (2 * tq, LANES), 1)

    for i in range(nb):
        q2 = q_ref[i * tq:(i + 1) * tq, :]
        qf = jnp.concatenate([jnp.where(lane < HEAD_DIM, q2, 0.0),
                              jnp.where(lane >= HEAD_DIM, q2, 0.0)], axis=0)
        qs = qf.astype(BF)

        def scores(j):
            return _dot_nt(qs, k_ref[j * tq:(j + 1) * tq, :]) * SCALE

        pieces = []
        if i > MOBA_TOPK:
            gate = lax.dot_general(qf, km, _NT, precision=lax.Precision.HIGHEST,
                                   preferred_element_type=F32)
            gm = jnp.where(jidx < i, gate, NINF)
            selpen = jnp.where(gm >= _third_largest(gm), 0.0, NEG)
            pen = lambda j: selpen[:, j:j + 1]
        else:
            pen = lambda j: 0.0
        for j in range(i - 1):
            pieces.append(scores(j) + (c31 + pen(j)))
        if i >= 1:
            pieces.append(scores(i - 1) + (bp_ref[0] + pen(i - 1)))
        pieces.append(scores(i) + bo_ref[0])

        m = functools.reduce(jnp.maximum, [jnp.max(s, axis=1, keepdims=True) for s in pieces])
        l = None
        acc = None
        for j, s in enumerate(pieces):
            p = jnp.exp(s - m)
            ls = jnp.sum(p, axis=1, keepdims=True)
            o = _dot(p.astype(BF), v_ref[j * tq:(j + 1) * tq, :])
            l = ls if l is None else l + ls
            acc = o if acc is None else acc + o
        out = acc / l
        o_ref[i * tq:(i + 1) * tq, :] = jnp.where(lane < HEAD_DIM, out[:tq], out[tq:]).astype(o_ref.dtype)


def _attn_prompt(q, kb, vb, kmean, bias_own, bias_prev, c31, batch, seq):
    n = q.shape[0]
    nb = seq // MOBA_BLOCK
    tq = MOBA_BLOCK
    grid_spec = pltpu.PrefetchScalarGridSpec(
        num_scalar_prefetch=1,
        grid=(batch, N_HEADS // 2),
        in_specs=[
            pl.BlockSpec((seq, LANES), lambda b, hp, c: (b, hp)),
            pl.BlockSpec((seq, LANES), lambda b, hp, c: (b, hp)),
            pl.BlockSpec((seq, LANES), lambda b, hp, c: (b, hp)),
            pl.BlockSpec((1, nb, LANES), lambda b, hp, c: (b, 0, hp)),
            pl.BlockSpec((1, 2 * tq, tq), lambda b, hp, c: (hp, 0, 0)),
            pl.BlockSpec((1, 2 * tq, tq), lambda b, hp, c: (hp, 0, 0)),
        ],
        out_specs=pl.BlockSpec((seq, LANES), lambda b, hp, c: (b, hp)),
    )
    return pl.pallas_call(
        functools.partial(_attn_prompt_kernel, nb=nb),
        grid_spec=grid_spec,
        out_shape=jax.ShapeDtypeStruct((n, D_MODEL), BF),
        compiler_params=_cparams("arbitrary", "arbitrary"),
        name="attn_prompt",
    )(c31, q, kb, vb, kmean.reshape(batch, nb, D_MODEL), bias_own, bias_prev)


SAMPLE_BLOCKS_PER_STEP = 4


def _attn_sample_kernel(pt_ref, q_ref, kn_ref, vn_ref, *refs, nblk, ds, ppb):
    bps = SAMPLE_BLOCKS_PER_STEP
    npg = bps * ppb
    k_refs = refs[:npg]
    v_refs = refs[npg:2 * npg]
    bias_ref, bown_ref, o_ref, qbd, o_scr, m_scr, l_scr, g_scr = refs[2 * npg:]
    step = pl.program_id(1)
    rows = N_HEADS * ds
    row = lax.broadcasted_iota(jnp.int32, (rows, D_MODEL), 0)
    col = lax.broadcasted_iota(jnp.int32, (rows, D_MODEL), 1)
    own = (col // HEAD_DIM) == (row // ds)

    @pl.when(step == 0)
    def _():
        qt = jnp.concatenate([q_ref[0]] * N_HEADS, axis=0)
        qbd[...] = jnp.where(own, qt, 0.0).astype(BF)

    qb = qbd[...]
    for bl in range(bps):
        blk = step * bps + bl
        kts = [k_refs[bl * ppb + pg][0] for pg in range(ppb)]
        vts = [v_refs[bl * ppb + pg][0] for pg in range(ppb)]
        s = jnp.concatenate([_dot(qb, kt.astype(BF)) for kt in kts], axis=1)
        gate = jnp.sum(s, axis=1, keepdims=True) * (1.0 / MOBA_BLOCK)
        s = s * SCALE + bias_ref[bl]
        m = jnp.max(s, axis=1, keepdims=True)
        p = jnp.exp(s - m)
        pb = p.astype(BF)
        o_scr[blk] = functools.reduce(jnp.add, [
            _dot_nt(pb[:, pg * PAGE_SIZE:(pg + 1) * PAGE_SIZE], vts[pg].astype(BF)) for pg in range(ppb)])
        m_scr[blk] = jnp.broadcast_to(m, (rows, LANES))
        l_scr[blk] = jnp.broadcast_to(jnp.sum(p, axis=1, keepdims=True), (rows, LANES))
        g_scr[blk] = jnp.broadcast_to(gate, (rows, LANES))

    @pl.when(step == pl.num_programs(1) - 1)
    def _():
        pad = jnp.zeros((LANES - ds, D_MODEL), F32)
        knb = jnp.concatenate([kn_ref[0], pad], axis=0).astype(BF)
        vnb = jnp.concatenate([vn_ref[0], pad], axis=0).astype(BF)
        so = _dot_nt(qb, knb) * SCALE + bown_ref[...]
        mo = jnp.max(so, axis=1, keepdims=True)
        po = jnp.exp(so - mo)
        lo = jnp.broadcast_to(jnp.sum(po, axis=1, keepdims=True), (rows, LANES))
        mo = jnp.broadcast_to(mo, (rows, LANES))
        oo = _dot(po.astype(BF), vnb)

        gs = [g_scr[b] for b in range(nblk)]
        g1 = functools.reduce(jnp.maximum, gs)
        gs2 = [jnp.where(g == g1, NINF, g) for g in gs]
        g2 = functools.reduce(jnp.maximum, gs2)
        gs3 = [jnp.where(g == g2, NINF, g) for g in gs2]
        g3 = functools.reduce(jnp.maximum, gs3)
        sel = [g >= g3 for g in gs]

        mm = mo
        for b in range(nblk):
            mm = jnp.maximum(mm, jnp.where(sel[b], m_scr[b], NINF))
        wide = lambda w: jnp.concatenate([w] * (D_MODEL // LANES), axis=1)
        wo = jnp.exp(mo - mm)
        lsum = lo * wo
        acc = oo * wide(wo)
        for b in range(nblk):
            wb = jnp.where(sel[b], jnp.exp(m_scr[b] - mm), 0.0)
            lsum = lsum + l_scr[b] * wb
            acc = acc + o_scr[b] * wide(wb)
        res = jnp.where(own, acc * wide(1.0 / lsum), 0.0)
        out = res[0:ds]
        for hh in range(1, N_HEADS):
            out = out + res[hh * ds:(hh + 1) * ds]
        o_ref[0] = out


def _attn_sample(q, kn, vn, cache_k, cache_v, page_table, bias_blk, bias_own):
    db, ds, _ = q.shape
    n_pages = page_table.shape[1]
    ppb = MOBA_BLOCK // PAGE_SIZE
    nblk = n_pages // ppb
    bps = SAMPLE_BLOCKS_PER_STEP
    npg = bps * ppb
    rows = N_HEADS * ds
    n_pool = cache_k.shape[0]
    ck = cache_k.transpose(0, 2, 3, 1).reshape(n_pool, D_MODEL, PAGE_SIZE)
    cv = cache_v.transpose(0, 2, 3, 1).reshape(n_pool, D_MODEL, PAGE_SIZE)
    pt = page_table.reshape(-1)

    def page_spec(k):
        return pl.BlockSpec((1, D_MODEL, PAGE_SIZE),
                            lambda s, j, pt_ref: (pt_ref[s * n_pages + j * npg + k], 0, 0))

    seq_spec = pl.BlockSpec((1, ds, D_MODEL), lambda s, j, pt_ref: (s, 0, 0))
    pages = [page_spec(k) for k in range(npg)]
    grid_spec = pltpu.PrefetchScalarGridSpec(
        num_scalar_prefetch=1,
        grid=(db, nblk // bps),
        in_specs=[seq_spec, seq_spec, seq_spec] + pages + pages + [
            pl.BlockSpec((bps, rows, MOBA_BLOCK), lambda s, j, pt_ref: (j, 0, 0)),
            pl.BlockSpec((rows, LANES), lambda s, j, pt_ref: (0, 0))],
        out_specs=seq_spec,
        scratch_shapes=[pltpu.VMEM((rows, D_MODEL), BF),
                        pltpu.VMEM((nblk, rows, D_MODEL), F32),
                        pltpu.VMEM((nblk, rows, LANES), F32),
                        pltpu.VMEM((nblk, rows, LANES), F32),
                        pltpu.VMEM((nblk, rows, LANES), F32)],
    )
    return pl.pallas_call(
        functools.partial(_attn_sample_kernel, nblk=nblk, ds=ds, ppb=ppb),
        grid_spec=grid_spec,
        out_shape=jax.ShapeDtypeStruct((db, ds, D_MODEL), F32),
        compiler_params=_cparams("arbitrary", "arbitrary"),
        name="attn_sample",
    )(pt, q, kn, vn, *([ck] * npg), *([cv] * npg), bias_blk, bias_own)


def _proj_res_kernel(x_ref, a_ref, g_ref, w_ref, o_ref):
    o_ref[...] = x_ref[...] + g_ref[0] * _dot(a_ref[...].astype(BF), w_ref[...])


def _proj_res(x, a, g, w, t_len, tm):
    n = x.shape[0]
    g_a, g_s = _mod_operand(g, t_len, tm)
    tile = pl.BlockSpec((tm, D_MODEL), lambda i: (i, 0))
    return pl.pallas_call(
        _proj_res_kernel,
        grid=(n // tm,),
        in_specs=[tile, tile, g_s, pl.BlockSpec((D_MODEL, D_MODEL), lambda i: (0, 0))],
        out_specs=tile,
        out_shape=jax.ShapeDtypeStruct((n, D_MODEL), F32),
        compiler_params=_cparams("arbitrary"),
        name="proj_res",
    )(x, a, g_a, w)


def _final_kernel(x_ref, pt_ref, g2_ref, fg_ref, o_ref):
    x = x_ref[...] + g2_ref[0] * pt_ref[...].T
    o_ref[...] = _rms(x, fg_ref[...])


def _final(x, peer_t, g2, fg, t_len, tm):
    n = x.shape[0]
    g_a, g_s = _mod_operand(g2, t_len, tm)
    tile = pl.BlockSpec((tm, D_MODEL), lambda i: (i, 0))
    return pl.pallas_call(
        _final_kernel,
        grid=(n // tm,),
        in_specs=[tile, pl.BlockSpec((D_MODEL, tm), lambda i: (0, i)), g_s,
                  pl.BlockSpec((1, D_MODEL), lambda i: (0, 0))],
        out_specs=tile,
        out_shape=jax.ShapeDtypeStruct((n, D_MODEL), F32),
        compiler_params=_cparams("arbitrary"),
        name="final_norm",
    )(x, peer_t, g_a, fg)


def _transpose_kernel(x_ref, o_ref):
    o_ref[0] = x_ref[0].T.astype(BF)


def _transpose_bf16(x, tr=512):
    nl, r, c = x.shape
    return pl.pallas_call(
        _transpose_kernel,
        grid=(nl, r // tr),
        in_specs=[pl.BlockSpec((1, tr, c), lambda l, i: (l, i, 0))],
        out_specs=pl.BlockSpec((1, c, tr), lambda l, i: (l, 0, i)),
        out_shape=jax.ShapeDtypeStruct((nl, c, r), BF),
        compiler_params=_cparams("arbitrary", "arbitrary"),
        name="transpose_bf16",
    )(x)


def _t5_bucket(rel):
    n = jnp.maximum(rel, 0)
    max_exact = N_BUCKETS // 2
    nf = jnp.maximum(n, max_exact).astype(F32)
    large = max_exact + (jnp.log(nf / max_exact) / math.log(MAX_DISTANCE / max_exact)
                         * (N_BUCKETS - max_exact)).astype(jnp.int32)
    large = jnp.minimum(large, N_BUCKETS - 1)
    return jnp.where(n < max_exact, n, large)


def _bias_table(rel, rel_bias, causal):
    onehot = (_t5_bucket(rel)[..., None] == jnp.arange(N_BUCKETS, dtype=jnp.int32)).astype(F32)
    b = jnp.einsum("rcb,bh->hrc", onehot, rel_bias, precision=lax.Precision.HIGHEST)
    if causal:
        b = jnp.where((rel >= 0)[None], b, NEG)
    return b


def _prompt_bias(rel_bias):
    t = jnp.arange(MOBA_BLOCK, dtype=jnp.int32)
    d = t[:, None] - t[None, :]
    own = _bias_table(d, rel_bias, True)
    prev = _bias_table(d + MOBA_BLOCK, rel_bias, False)
    pair = lambda b: b.reshape(N_HEADS // 2, 2 * MOBA_BLOCK, MOBA_BLOCK)
    far = rel_bias[_t5_bucket(jnp.int32(MOBA_BLOCK + 1))]
    return pair(own), pair(prev), far


def _sample_bias(rel_bias, ds, past, nblk):
    t = jnp.arange(ds, dtype=jnp.int32)
    l = jnp.arange(MOBA_BLOCK, dtype=jnp.int32)
    blocks = []
    for j in range(nblk):
        rel = past + t[:, None] - (j * MOBA_BLOCK + l[None, :])
        blocks.append(_bias_table(rel, rel_bias, False).reshape(N_HEADS * ds, MOBA_BLOCK))
    own = _bias_table(t[:, None] - t[None, :], rel_bias, True).reshape(N_HEADS * ds, ds)
    own = jnp.concatenate([own, jnp.full((N_HEADS * ds, LANES - ds), NEG, F32)], axis=1)
    return jnp.stack(blocks), own


def _gate_weights(w_s, b_s, chunk_len):
    w = jnp.where(jnp.tril(jnp.ones((chunk_len, chunk_len), bool)), w_s[:, :chunk_len, :chunk_len], 0)
    reps = CHUNK // chunk_len
    if reps > 1:
        eye = jnp.eye(reps, dtype=w.dtype)
        w = jnp.einsum("ab,gts->gatbs", eye, w).reshape(A_GROUPS, CHUNK, CHUNK)
    bias = jnp.tile(b_s[:, :chunk_len].T, (reps, 1))
    bias = jnp.repeat(bias, D_MODEL // A_GROUPS, axis=1)
    return w.astype(BF), bias


def kernel(x_prompt, x_sample, cache_k, cache_v, page_table, c_prompt, c_sample, a_w_in, a_g_v, a_w_s, a_b_s, a_w_out, b_w_q, b_w_o, kv_g, w_kv, rel_bias, g_mix, g_ffn, w_ada, b_ada, p_w_q, p_sub_keys, p_w_u, p_w_v, final_g):
    batch, seq, _ = x_prompt.shape
    db, ds, _ = x_sample.shape
    n_p = batch * seq
    n_s = db * ds
    past = page_table.shape[1] * PAGE_SIZE
    row = lambda v: v.reshape(1, D_MODEL)

    w_in = a_w_in[0].astype(BF)
    w_out = a_w_out[0].astype(BF)
    wkv = w_kv.astype(BF)
    wq_attn = b_w_q[0].astype(BF)
    wo_attn = b_w_o[0].astype(BF)
    wqt = _transpose_bf16(p_w_q)
    sk = p_sub_keys.astype(BF)
    wu = p_w_u.astype(BF)
    wvt = _transpose_bf16(p_w_v)

    mod = _ada(jnp.concatenate([c_prompt, c_sample], axis=0), w_ada, b_ada)

    def mods(l, lo, hi):
        m = mod[l, lo:hi]
        return [m[:, k * D_MODEL:(k + 1) * D_MODEL] for k in range(6)]

    bias_own_p, bias_prev_p, c31 = _prompt_bias(rel_bias)
    bias_blk_s, bias_own_s = _sample_bias(rel_bias, ds, past, past // MOBA_BLOCK)

    def trunk(x, t_len, lo, hi, gate_w, gate_b, tm, attend):
        sh1, sc1, g1, sh2, sc2, g2 = mods(0, lo, hi)
        x1, v_rows = _gmlp(x, row(g_mix[0]), sc1, sh1, g1, w_in, row(a_g_v[0]), gate_w, gate_b, w_out,
                           t_len, tm)
        peer_t = _peer(x1, row(g_ffn[0]), sc2, sh2, wqt, sk, wu, wvt, 0, t_len)
        sh1b, sc1b, g1b, sh2b, sc2b, g2b = mods(1, lo, hi)
        x2, k, v, kb, vb, kmean, q = _kvq(x1, peer_t, g2, row(kv_g), wkv, row(g_mix[1]), sc1b, sh1b,
                                          wq_attn, t_len)
        att = attend(q, k, v, kb, vb, kmean)
        x3 = _proj_res(x2, att, g1b, wo_attn, t_len, MOBA_BLOCK)
        peer_t = _peer(x3, row(g_ffn[1]), sc2b, sh2b, wqt, sk, wu, wvt, 1, t_len)
        y = _final(x3, peer_t, g2b, row(final_g), t_len, MOBA_BLOCK)
        return y, k, v, v_rows

    def attend_prompt(q, k, v, kb, vb, kmean):
        return _attn_prompt(q, kb, vb, kmean, bias_own_p, bias_prev_p, c31, batch, seq)

    def attend_sample(q, k, v, kb, vb, kmean):
        out = _attn_sample(q.reshape(db, ds, D_MODEL), k.reshape(db, ds, D_MODEL), v.reshape(db, ds, D_MODEL),
                           cache_k, cache_v, page_table, bias_blk_s, bias_own_s)
        return out.reshape(n_s, D_MODEL)

    gw_p, gb_p = _gate_weights(a_w_s[0], a_b_s[0], min(seq, CHUNK))
    gw_s, gb_s = _gate_weights(a_w_s[0], a_b_s[0], min(ds, CHUNK))

    y_p, k_p, v_p, _ = trunk(x_prompt.reshape(n_p, D_MODEL), seq, 0, batch, gw_p, gb_p, 512, attend_prompt)
    y_s, k_s, v_s, cv_s = trunk(x_sample.reshape(n_s, D_MODEL), ds, batch, batch + db, gw_s, gb_s, 128,
                                attend_sample)

    heads = lambda a, b, t: a.reshape(b, t, N_HEADS, HEAD_DIM)
    return (y_p.reshape(batch, seq, D_MODEL), y_s.reshape(db, ds, D_MODEL),
            heads(k_p, batch, seq), heads(v_p, batch, seq),
            heads(k_s, db, ds), heads(v_s, db, ds),
            cv_s.reshape(1, db, ds, D_MODEL))
```

```python
import functools
import math

import jax
import jax.numpy as jnp
from jax import lax
from jax.experimental import pallas as pl
from jax.experimental.pallas import tpu as pltpu

D_MODEL = 1024
N_HEADS = 16
HEAD_DIM = 64
CHUNK = 128
A_GROUPS = 8
MOBA_BLOCK = 256
MOBA_TOPK = 3
PAGE_SIZE = 128
N_BUCKETS = 32
MAX_DISTANCE = 128
PEER_HEADS = 8
PEER_NKEYS = 128
PEER_TOPK = 16
EPS = 1e-6
NEG = -1e30
NINF = float("-inf")
SCALE = HEAD_DIM ** -0.5

BF = jnp.bfloat16
F32 = jnp.float32

LANES = 128
VMEM_LIMIT = 56 * 1024 * 1024

_NT = (((1,), (1,)), ((), ()))


def _cparams(*sem):
    return pltpu.CompilerParams(dimension_semantics=sem, vmem_limit_bytes=VMEM_LIMIT)


def _rms(x, g):
    ms = jnp.mean(x * x, axis=-1, keepdims=True)
    return x * lax.rsqrt(ms + EPS) * g


def _dot(a, b):
    return jnp.dot(a, b, preferred_element_type=F32)


def _dot_nt(a, b):
    return lax.dot_general(a, b, _NT, preferred_element_type=F32)


def _ada_kernel(c_ref, w_ref, b_ref, o_ref):
    c = c_ref[...]
    s = (c * jax.nn.sigmoid(c)).astype(BF)
    o_ref[0] = _dot(s, w_ref[0].astype(BF)) + b_ref[0]


def _ada(c_all, w_ada, b_ada):
    m = c_all.shape[0]
    depth, _, n6 = w_ada.shape
    tn = 1536
    return pl.pallas_call(
        _ada_kernel,
        grid=(depth, n6 // tn),
        in_specs=[
            pl.BlockSpec((m, D_MODEL), lambda l, j: (0, 0)),
            pl.BlockSpec((1, D_MODEL, tn), lambda l, j: (l, 0, j)),
            pl.BlockSpec((1, 1, tn), lambda l, j: (l, 0, j)),
        ],
        out_specs=pl.BlockSpec((1, m, tn), lambda l, j: (l, 0, j)),
        out_shape=jax.ShapeDtypeStruct((depth, m, n6), F32),
        compiler_params=_cparams("arbitrary", "arbitrary"),
        name="ada",
    )(c_all, w_ada, b_ada.reshape(depth, 1, n6))


def _mod_operand(arr, t_len, tm):
    b = arr.shape[0]
    if t_len % tm == 0:
        per = t_len // tm
        return arr.reshape(b, 1, D_MODEL), pl.BlockSpec((1, 1, D_MODEL), lambda i, *_: (i // per, 0, 0))
    n = b * t_len
    rep = jnp.repeat(arr, t_len, axis=0).reshape(n // tm, tm, D_MODEL)
    return rep, pl.BlockSpec((1, tm, D_MODEL), lambda i, *_: (i, 0, 0))


def _gmlp_kernel(x_ref, gm_ref, sc_ref, sh_ref, g1_ref, win_ref, gv_ref, wg_ref, bs_ref, wout_ref,
                 xo_ref, v_ref, us_scr, *, tm):
    x = x_ref[...]
    h = _rms(x, gm_ref[...]) * (1.0 + sc_ref[0]) + sh_ref[0]
    uv = jax.nn.gelu(_dot(h.astype(BF), win_ref[...]))
    u = uv[:, :D_MODEL]
    v = _rms(uv[:, D_MODEL:], gv_ref[...])
    v_ref[...] = v
    vb = v.astype(BF)
    for c in range(tm // CHUNK):
        r0 = c * CHUNK
        parts = []
        for g in range(A_GROUPS):
            c0 = g * CHUNK
            parts.append(_dot(wg_ref[g], vb[r0:r0 + CHUNK, c0:c0 + CHUNK]))
        s = jnp.concatenate(parts, axis=1) + bs_ref[...]
        us_scr[r0:r0 + CHUNK, :] = (u[r0:r0 + CHUNK, :] * s).astype(BF)
    out = _dot(us_scr[...], wout_ref[...])
    xo_ref[...] = x + g1_ref[0] * out


def _gmlp(x, gm, sc, sh, g1, w_in, g_v, wg, bs, w_out, t_len, tm):
    n = x.shape[0]
    sc_a, sc_s = _mod_operand(sc, t_len, tm)
    sh_a, sh_s = _mod_operand(sh, t_len, tm)
    g1_a, g1_s = _mod_operand(g1, t_len, tm)
    full = lambda shape: pl.BlockSpec(shape, lambda i: (0,) * len(shape))
    return pl.pallas_call(
        functools.partial(_gmlp_kernel, tm=tm),
        grid=(n // tm,),
        in_specs=[
            pl.BlockSpec((tm, D_MODEL), lambda i: (i, 0)),
            full((1, D_MODEL)), sc_s, sh_s, g1_s,
            full((D_MODEL, 2 * D_MODEL)), full((1, D_MODEL)),
            full((A_GROUPS, CHUNK, CHUNK)), full((CHUNK, D_MODEL)),
            full((D_MODEL, D_MODEL)),
        ],
        out_specs=[pl.BlockSpec((tm, D_MODEL), lambda i: (i, 0)),
                   pl.BlockSpec((tm, D_MODEL), lambda i: (i, 0))],
        out_shape=[jax.ShapeDtypeStruct((n, D_MODEL), F32), jax.ShapeDtypeStruct((n, D_MODEL), F32)],
        scratch_shapes=[pltpu.VMEM((tm, D_MODEL), BF)],
        compiler_params=_cparams("arbitrary"),
        name="gmlp",
    )(x, gm, sc_a, sh_a, g1_a, w_in, g_v, wg, bs, w_out)


SUBLANES = 8


def _oddeven_sort_pairs(n):
    pairs = []
    p = 1
    while p < n:
        k = p
        while k >= 1:
            for j in range(k % p, n - k, 2 * k):
                for i in range(min(k, n - j - k)):
                    if (i + j) // (2 * p) == (i + j + k) // (2 * p):
                        pairs.append((i + j, i + j + k))
            k //= 2
        p *= 2
    return pairs


_SORT16 = _oddeven_sort_pairs(PEER_TOPK)
_BITONIC16 = [(i, i | k) for k in (8, 4, 2, 1) for i in range(PEER_TOPK) if not i & k]


def _cex(v, pairs):
    for i, j in pairs:
        v[i], v[j] = jnp.maximum(v[i], v[j]), jnp.minimum(v[i], v[j])


def _top16_sorted(v):
    v = list(v)
    _cex(v, _SORT16)
    for shift in (4, 2, 1):
        w = [pltpu.roll(x, shift, axis=0) for x in v]
        v = [jnp.maximum(v[k], w[PEER_TOPK - 1 - k]) for k in range(PEER_TOPK)]
        _cex(v, _BITONIC16)
    return v


def _kth_sum(a, b):
    shape = a[0].shape
    row = lax.broadcasted_iota(jnp.int32, shape, 0)

    def stack(xs):
        out = xs[SUBLANES - 1]
        for r in range(SUBLANES - 2, -1, -1):
            out = jnp.where(row == r, xs[r], out)
        return out

    a_lo, a_hi, b_hi = stack(a[:SUBLANES]), stack(a[SUBLANES:]), stack(b[SUBLANES:])
    cands = [a_lo + b[0], a_hi + b[0]]
    for l in range(1, SUBLANES):
        kmax = PEER_TOPK // (l + 1)
        c = a_lo + b[l]
        if kmax < SUBLANES:
            c = jnp.where(row < kmax, c, NINF)
        cands.append(c)
    cands.append(b_hi + a[0])
    cands += [jnp.full(shape, NINF, F32)] * (PEER_TOPK - len(cands))
    return _top16_sorted(cands)


def _route_kernel(x_ref, gm_ref, sc_ref, sh_ref, wqt_ref, sk_ref,
                  ht_ref, s0_ref, s1_ref, e0_ref, e1_ref, thr_ref,
                  qt_scr, *, tn):
    x = x_ref[...]
    h = _rms(x, gm_ref[...]) * (1.0 + sc_ref[0]) + sh_ref[0]
    ht = h.T.astype(BF)
    ht_ref[...] = ht
    qt_scr[...] = _dot(wqt_ref[...], ht).astype(BF)

    def head(hd, carry):
        st = []
        for p in range(2):
            r0 = pl.multiple_of((hd * 2 + p) * PEER_NKEYS, PEER_NKEYS)
            st.append(_dot(sk_ref[p, hd], qt_scr[pl.ds(r0, PEER_NKEYS), :]))
        for tb in range(tn // LANES):
            lanes = slice(tb * LANES, (tb + 1) * LANES)
            s0 = st[0][:, lanes]
            s1 = st[1][:, lanes]
            s0_ref[tb, hd] = s0
            s1_ref[tb, hd] = s1
            tiles = lambda s: [s[SUBLANES * r:SUBLANES * (r + 1)] for r in range(PEER_NKEYS // SUBLANES)]
            a = _top16_sorted(tiles(s0))
            b = _top16_sorted(tiles(s1))
            top = _kth_sum(a, b)
            z = functools.reduce(jnp.add, [jnp.exp(t - top[0]) for t in top[1:]]) + 1.0
            thr_ref[tb, hd] = top[PEER_TOPK - 1][0:1]
            e0_ref[tb, hd] = jnp.exp(s0 - a[0][0:1]) * (1.0 / z[0:1])
            e1_ref[tb, hd] = jnp.exp(s1 - b[0][0:1])
        return carry

    lax.fori_loop(0, PEER_HEADS, head, 0)


def _route(x, gm, sc, sh, wqt, sk, layer, t_len, tn):
    n = x.shape[0]
    sc_a, sc_s = _mod_operand(sc, t_len, tn)
    sh_a, sh_s = _mod_operand(sh, t_len, tn)
    full = lambda shape: pl.BlockSpec(shape, lambda i: (0,) * len(shape))
    nlb = tn // LANES
    sshape = jax.ShapeDtypeStruct((n // LANES, PEER_HEADS, PEER_NKEYS, LANES), F32)
    sspec = pl.BlockSpec((nlb, PEER_HEADS, PEER_NKEYS, LANES), lambda i: (i, 0, 0, 0))
    return pl.pallas_call(
        functools.partial(_route_kernel, tn=tn),
        grid=(n // tn,),
        in_specs=[
            pl.BlockSpec((tn, D_MODEL), lambda i: (i, 0)),
            full((1, D_MODEL)), sc_s, sh_s,
            pl.BlockSpec((None, 2 * PEER_HEADS * PEER_NKEYS, D_MODEL), lambda i: (layer, 0, 0)),
            pl.BlockSpec((None, 2, PEER_HEADS, PEER_NKEYS, PEER_NKEYS), lambda i: (layer, 0, 0, 0, 0)),
        ],
        out_specs=[pl.BlockSpec((D_MODEL, tn), lambda i: (0, i)), sspec, sspec, sspec, sspec,
                   pl.BlockSpec((nlb, PEER_HEADS, 1, LANES), lambda i: (i, 0, 0, 0))],
        out_shape=[jax.ShapeDtypeStruct((D_MODEL, n), BF), sshape, sshape, sshape, sshape,
                   jax.ShapeDtypeStruct((n // LANES, PEER_HEADS, 1, LANES), F32)],
        scratch_shapes=[pltpu.VMEM((2 * PEER_HEADS * PEER_NKEYS, tn), BF)],
        compiler_params=_cparams("arbitrary"),
        name="peer_route",
    )(x, gm, sc_a, sh_a, wqt, sk)


PEER_I_PER_STEP = 8


def _expert_kernel(ht_ref, s1_ref, e1_ref, s0_ref, e0_ref, thr_ref, wu_ref, wvt_ref, o_ref,
                   ga_scr, a_scr, *, tn):
    e = pl.program_id(1)
    a_scr[...] = _dot(wu_ref[...], ht_ref[...])
    jr = 16
    bcast = lambda r: jnp.broadcast_to(r, (jr, LANES))
    for tb in range(tn // LANES):
        lanes = slice(tb * LANES, (tb + 1) * LANES)

        def jgroup(jg, carry, tb=tb, lanes=lanes):
            j0 = pl.multiple_of(jg * jr, jr)
            g = [None] * PEER_I_PER_STEP
            for hd in range(PEER_HEADS):
                s1 = s1_ref[tb, hd, pl.ds(j0, jr), :]
                e1 = e1_ref[tb, hd, pl.ds(j0, jr), :]
                thr = bcast(thr_ref[tb, hd])
                for il in range(PEER_I_PER_STEP):
                    val = s1 + bcast(s0_ref[tb, hd, il:il + 1, :])
                    w = e1 * bcast(e0_ref[tb, hd, il:il + 1, :])
                    t = jnp.where(val >= thr, w, 0.0)
                    g[il] = t if g[il] is None else g[il] + t
            for il in range(PEER_I_PER_STEP):
                r0 = pl.multiple_of(il * PEER_NKEYS + j0, jr)
                ga_scr[pl.ds(r0, jr), lanes] = (jax.nn.gelu(a_scr[pl.ds(r0, jr), lanes]) * g[il]).astype(BF)
            return carry

        lax.fori_loop(0, PEER_NKEYS // jr, jgroup, 0)
    part = _dot(wvt_ref[...], ga_scr[...])

    @pl.when(e == 0)
    def _():
        o_ref[...] = part

    @pl.when(e > 0)
    def _():
        o_ref[...] += part


def _experts(ht, s0, s1, e0, e1, thr, wu, wvt, layer, tn):
    n = ht.shape[1]
    n_exp = wu.shape[1]
    te = PEER_I_PER_STEP * PEER_NKEYS
    nlb = tn // LANES
    return pl.pallas_call(
        functools.partial(_expert_kernel, tn=tn),
        grid=(n // tn, n_exp // te),
        in_specs=[
            pl.BlockSpec((D_MODEL, tn), lambda t, e: (0, t)),
            pl.BlockSpec((nlb, PEER_HEADS, PEER_NKEYS, LANES), lambda t, e: (t, 0, 0, 0)),
            pl.BlockSpec((nlb, PEER_HEADS, PEER_NKEYS, LANES), lambda t, e: (t, 0, 0, 0)),
            pl.BlockSpec((nlb, PEER_HEADS, PEER_I_PER_STEP, LANES), lambda t, e: (t, 0, e, 0)),
            pl.BlockSpec((nlb, PEER_HEADS, PEER_I_PER_STEP, LANES), lambda t, e: (t, 0, e, 0)),
            pl.BlockSpec((nlb, PEER_HEADS, 1, LANES), lambda t, e: (t, 0, 0, 0)),
            pl.BlockSpec((None, te, D_MODEL), lambda t, e: (layer, e, 0)),
            pl.BlockSpec((None, D_MODEL, te), lambda t, e: (layer, 0, e)),
        ],
        out_specs=pl.BlockSpec((D_MODEL, tn), lambda t, e: (0, t)),
        out_shape=jax.ShapeDtypeStruct((D_MODEL, n), F32),
        scratch_shapes=[pltpu.VMEM((te, tn), BF), pltpu.VMEM((te, tn), F32)],
        compiler_params=_cparams("arbitrary", "arbitrary"),
        name="peer_experts",
    )(ht, s1, e1, s0, e0, thr, wu, wvt)


def _peer(x, gm, sc, sh, wqt, sk, wu, wvt, layer, t_len):
    n = x.shape[0]
    ht, s0, s1, e0, e1, thr = _route(x, gm, sc, sh, wqt, sk, layer, t_len, 256)
    return _experts(ht, s0, s1, e0, e1, thr, wu, wvt, layer, min(1024, n))


def _kvq_kernel(x_ref, pt_ref, g2_ref, kvg_ref, wkv_ref, gm_ref, sc_ref, sh_ref, wq_ref,
                xo_ref, k_ref, v_ref, kb_ref, vb_ref, km_ref, q_ref, *, kv_transposed):
    x = x_ref[...] + g2_ref[0] * pt_ref[...].T
    xo_ref[...] = x
    kv = _dot(_rms(x, kvg_ref[...]).astype(BF), wkv_ref[...])
    k = kv[:, :D_MODEL]
    v = kv[:, D_MODEL:]
    if kv_transposed:
        k_ref[0] = k.T
        v_ref[0] = v.T
    else:
        k_ref[...] = k
        v_ref[...] = v
    kb_ref[...] = k.astype(BF)
    vb_ref[...] = v.astype(BF)
    km_ref[0] = jnp.mean(k, axis=0, keepdims=True)
    h = _rms(x, gm_ref[...]) * (1.0 + sc_ref[0]) + sh_ref[0]
    q_ref[...] = _dot(h.astype(BF), wq_ref[...])


def _kvq(x, peer_t, g2, kv_g, w_kv, gm, sc, sh, w_q, t_len, kv_transposed):
    n = x.shape[0]
    tm = MOBA_BLOCK
    g2_a, g2_s = _mod_operand(g2, t_len, tm)
    sc_a, sc_s = _mod_operand(sc, t_len, tm)
    sh_a, sh_s = _mod_operand(sh, t_len, tm)
    full = lambda shape: pl.BlockSpec(shape, lambda i: (0,) * len(shape))
    tile = pl.BlockSpec((tm, D_MODEL), lambda i: (i, 0))
    f32 = jax.ShapeDtypeStruct((n, D_MODEL), F32)
    bf = jax.ShapeDtypeStruct((n, D_MODEL), BF)
    if kv_transposed:
        per = t_len // tm
        kv_shape = jax.ShapeDtypeStruct((n // t_len, D_MODEL, t_len), F32)
        kv_spec = pl.BlockSpec((1, D_MODEL, tm), lambda i: (i // per, 0, i % per))
    else:
        kv_shape, kv_spec = f32, tile
    return pl.pallas_call(
        functools.partial(_kvq_kernel, kv_transposed=kv_transposed),
        grid=(n // tm,),
        in_specs=[tile, pl.BlockSpec((D_MODEL, tm), lambda i: (0, i)), g2_s,
                  full((1, D_MODEL)), full((D_MODEL, 2 * D_MODEL)),
                  full((1, D_MODEL)), sc_s, sh_s, full((D_MODEL, D_MODEL))],
        out_specs=[tile, kv_spec, kv_spec, tile, tile,
                   pl.BlockSpec((1, 1, D_MODEL), lambda i: (i, 0, 0)), tile],
        out_shape=[f32, kv_shape, kv_shape, bf, bf, jax.ShapeDtypeStruct((n // tm, 1, D_MODEL), F32), f32],
        compiler_params=_cparams("arbitrary"),
        name="kvq",
    )(x, peer_t, g2_a, kv_g, w_kv, gm, sc_a, sh_a, w_q)


def _third_largest(g):
    m1 = jnp.max(g, axis=1, keepdims=True)
    g2 = jnp.where(g == m1, NINF, g)
    m2 = jnp.max(g2, axis=1, keepdims=True)
    g3 = jnp.where(g2 == m2, NINF, g2)
    return jnp.max(g3, axis=1, keepdims=True)


def _attn_prompt_kernel(c31_ref, q_ref, k_ref, v_ref, km_ref, bo_ref, bp_ref, o_ref, *, nb):
    hp = pl.program_id(1)
    tq = MOBA_BLOCK
    lane = lax.broadcasted_iota(jnp.int32, (tq, LANES), 1)
    row = lax.broadcasted_iota(jnp.int32, (2 * tq, 1), 0)
    c31 = jnp.where(row < tq, c31_ref[2 * hp], c31_ref[2 * hp + 1])
    km = jnp.concatenate([km_ref[0], jnp.zeros((LANES - nb, LANES), F32)], axis=0)
    jidx = lax.broadcasted_iota(jnp.int32, (2 * tq, LANES), 1)

    for i in range(nb):
        q2 = q_ref[i * tq:(i + 1) * tq, :]
        qf = jnp.concatenate([jnp.where(lane < HEAD_DIM, q2, 0.0),
                              jnp.where(lane >= HEAD_DIM, q2, 0.0)], axis=0)
        qs = qf.astype(BF)

        def scores(j):
            return _dot_nt(qs, k_ref[j * tq:(j + 1) * tq, :]) * SCALE

        pieces = []
        if i > MOBA_TOPK:
            gate = lax.dot_general(qf, km, _NT, precision=lax.Precision.HIGHEST,
                                   preferred_element_type=F32)
            gm = jnp.where(jidx < i, gate, NINF)
            selpen = jnp.where(gm >= _third_largest(gm), 0.0, NEG)
            pen = lambda j: selpen[:, j:j + 1]
        else:
            pen = lambda j: 0.0
        for j in range(i - 1):
            pieces.append(scores(j) + (c31 + pen(j)))
        if i >= 1:
            pieces.append(scores(i - 1) + (bp_ref[0] + pen(i - 1)))
        pieces.append(scores(i) + bo_ref[0])

        m = functools.reduce(jnp.maximum, [jnp.max(s, axis=1, keepdims=True) for s in pieces])
        l = None
        acc = None
        for j, s in enumerate(pieces):
            p = jnp.exp(s - m)
            ls = jnp.sum(p, axis=1, keepdims=True)
            o = _dot(p.astype(BF), v_ref[j * tq:(j + 1) * tq, :])
            l = ls if l is None else l + ls
            acc = o if acc is None else acc + o
        out = acc / l
        o_ref[i * tq:(i + 1) * tq, :] = jnp.where(lane < HEAD_DIM, out[:tq], out[tq:]).astype(o_ref.dtype)


def _attn_prompt(q, kb, vb, kmean, bias_own, bias_prev, c31, batch, seq):
    n = q.shape[0]
    nb = seq // MOBA_BLOCK
    tq = MOBA_BLOCK
    grid_spec = pltpu.PrefetchScalarGridSpec(
        num_scalar_prefetch=1,
        grid=(batch, N_HEADS // 2),
        in_specs=[
            pl.BlockSpec((seq, LANES), lambda b, hp, c: (b, hp)),
            pl.BlockSpec((seq, LANES), lambda b, hp, c: (b, hp)),
            pl.BlockSpec((seq, LANES), lambda b, hp, c: (b, hp)),
            pl.BlockSpec((1, nb, LANES), lambda b, hp, c: (b, 0, hp)),
            pl.BlockSpec((1, 2 * tq, tq), lambda b, hp, c: (hp, 0, 0)),
            pl.BlockSpec((1, 2 * tq, tq), lambda b, hp, c: (hp, 0, 0)),
        ],
        out_specs=pl.BlockSpec((seq, LANES), lambda b, hp, c: (b, hp)),
    )
    return pl.pallas_call(
        functools.partial(_attn_prompt_kernel, nb=nb),
        grid_spec=grid_spec,
        out_shape=jax.ShapeDtypeStruct((n, D_MODEL), BF),
        compiler_params=_cparams("arbitrary", "arbitrary"),
        name="attn_prompt",
    )(c31, q, kb, vb, kmean.reshape(batch, nb, D_MODEL), bias_own, bias_prev)


SAMPLE_BLOCKS_PER_STEP = 4


def _attn_sample_kernel(pt_ref, q_ref, kn_ref, vn_ref, *refs, nblk, ds, ppb):
    bps = SAMPLE_BLOCKS_PER_STEP
    npg = bps * ppb
    k_refs = refs[:npg]
    v_refs = refs[npg:2 * npg]
    bias_ref, bown_ref, o_ref, qbd, o_scr, m_scr, l_scr, g_scr = refs[2 * npg:]
    step = pl.program_id(1)
    rows = N_HEADS * ds
    row = lax.broadcasted_iota(jnp.int32, (rows, D_MODEL), 0)
    col = lax.broadcasted_iota(jnp.int32, (rows, D_MODEL), 1)
    own = (col // HEAD_DIM) == (row // ds)

    @pl.when(step == 0)
    def _():
        qt = jnp.concatenate([q_ref[0]] * N_HEADS, axis=0)
        qbd[...] = jnp.where(own, qt, 0.0).astype(BF)

    qb = qbd[...]
    for bl in range(bps):
        blk = step * bps + bl
        kts = [k_refs[bl * ppb + pg][0] for pg in range(ppb)]
        vts = [v_refs[bl * ppb + pg][0] for pg in range(ppb)]
        s = jnp.concatenate([_dot(qb, kt.astype(BF)) for kt in kts], axis=1)
        gate = jnp.sum(s, axis=1, keepdims=True) * (1.0 / MOBA_BLOCK)
        s = s * SCALE + bias_ref[bl]
        m = jnp.max(s, axis=1, keepdims=True)
        p = jnp.exp(s - m)
        pb = p.astype(BF)
        o_scr[blk] = functools.reduce(jnp.add, [
            _dot_nt(pb[:, pg * PAGE_SIZE:(pg + 1) * PAGE_SIZE], vts[pg].astype(BF)) for pg in range(ppb)])
        m_scr[blk] = jnp.broadcast_to(m, (rows, LANES))
        l_scr[blk] = jnp.broadcast_to(jnp.sum(p, axis=1, keepdims=True), (rows, LANES))
        g_scr[blk] = jnp.broadcast_to(gate, (rows, LANES))

    @pl.when(step == pl.num_programs(1) - 1)
    def _():
        pad = jnp.zeros((LANES - ds, D_MODEL), F32)
        knb = jnp.concatenate([kn_ref[0], pad], axis=0).astype(BF)
        vnb = jnp.concatenate([vn_ref[0], pad], axis=0).astype(BF)
        so = _dot_nt(qb, knb) * SCALE + bown_ref[...]
        mo = jnp.max(so, axis=1, keepdims=True)
        po = jnp.exp(so - mo)
        lo = jnp.broadcast_to(jnp.sum(po, axis=1, keepdims=True), (rows, LANES))
        mo = jnp.broadcast_to(mo, (rows, LANES))
        oo = _dot(po.astype(BF), vnb)

        gs = [g_scr[b] for b in range(nblk)]
        g1 = functools.reduce(jnp.maximum, gs)
        gs2 = [jnp.where(g == g1, NINF, g) for g in gs]
        g2 = functools.reduce(jnp.maximum, gs2)
        gs3 = [jnp.where(g == g2, NINF, g) for g in gs2]
        g3 = functools.reduce(jnp.maximum, gs3)
        sel = [g >= g3 for g in gs]

        mm = mo
        for b in range(nblk):
            mm = jnp.maximum(mm, jnp.where(sel[b], m_scr[b], NINF))
        wide = lambda w: jnp.concatenate([w] * (D_MODEL // LANES), axis=1)
        wo = jnp.exp(mo - mm)
        lsum = lo * wo
        acc = oo * wide(wo)
        for b in range(nblk):
            wb = jnp.where(sel[b], jnp.exp(m_scr[b] - mm), 0.0)
            lsum = lsum + l_scr[b] * wb
            acc = acc + o_scr[b] * wide(wb)
        res = jnp.where(own, acc * wide(1.0 / lsum), 0.0)
        out = res[0:ds]
        for hh in range(1, N_HEADS):
            out = out + res[hh * ds:(hh + 1) * ds]
        o_ref[0] = out


def _attn_sample(q, kn, vn, cache_k, cache_v, page_table, bias_blk, bias_own):
    db, ds, _ = q.shape
    n_pages = page_table.shape[1]
    ppb = MOBA_BLOCK // PAGE_SIZE
    nblk = n_pages // ppb
    bps = SAMPLE_BLOCKS_PER_STEP
    npg = bps * ppb
    rows = N_HEADS * ds
    n_pool = cache_k.shape[0]
    ck = cache_k.transpose(0, 2, 3, 1).reshape(n_pool, D_MODEL, PAGE_SIZE)
    cv = cache_v.transpose(0, 2, 3, 1).reshape(n_pool, D_MODEL, PAGE_SIZE)
    pt = page_table.reshape(-1)

    def page_spec(k):
        return pl.BlockSpec((1, D_MODEL, PAGE_SIZE),
                            lambda s, j, pt_ref: (pt_ref[s * n_pages + j * npg + k], 0, 0))

    seq_spec = pl.BlockSpec((1, ds, D_MODEL), lambda s, j, pt_ref: (s, 0, 0))
    pages = [page_spec(k) for k in range(npg)]
    grid_spec = pltpu.PrefetchScalarGridSpec(
        num_scalar_prefetch=1,
        grid=(db, nblk // bps),
        in_specs=[seq_spec, seq_spec, seq_spec] + pages + pages + [
            pl.BlockSpec((bps, rows, MOBA_BLOCK), lambda s, j, pt_ref: (j, 0, 0)),
            pl.BlockSpec((rows, LANES), lambda s, j, pt_ref: (0, 0))],
        out_specs=seq_spec,
        scratch_shapes=[pltpu.VMEM((rows, D_MODEL), BF),
                        pltpu.VMEM((nblk, rows, D_MODEL), F32),
                        pltpu.VMEM((nblk, rows, LANES), F32),
                        pltpu.VMEM((nblk, rows, LANES), F32),
                        pltpu.VMEM((nblk, rows, LANES), F32)],
    )
    return pl.pallas_call(
        functools.partial(_attn_sample_kernel, nblk=nblk, ds=ds, ppb=ppb),
        grid_spec=grid_spec,
        out_shape=jax.ShapeDtypeStruct((db, ds, D_MODEL), F32),
        compiler_params=_cparams("arbitrary", "arbitrary"),
        name="attn_sample",
    )(pt, q, kn, vn, *([ck] * npg), *([cv] * npg), bias_blk, bias_own)


def _proj_res_kernel(x_ref, a_ref, g_ref, w_ref, o_ref):
    o_ref[...] = x_ref[...] + g_ref[0] * _dot(a_ref[...].astype(BF), w_ref[...])


def _proj_res(x, a, g, w, t_len, tm):
    n = x.shape[0]
    g_a, g_s = _mod_operand(g, t_len, tm)
    tile = pl.BlockSpec((tm, D_MODEL), lambda i: (i, 0))
    return pl.pallas_call(
        _proj_res_kernel,
        grid=(n // tm,),
        in_specs=[tile, tile, g_s, pl.BlockSpec((D_MODEL, D_MODEL), lambda i: (0, 0))],
        out_specs=tile,
        out_shape=jax.ShapeDtypeStruct((n, D_MODEL), F32),
        compiler_params=_cparams("arbitrary"),
        name="proj_res",
    )(x, a, g_a, w)


def _final_kernel(x_ref, pt_ref, g2_ref, fg_ref, o_ref):
    x = x_ref[...] + g2_ref[0] * pt_ref[...].T
    o_ref[...] = _rms(x, fg_ref[...])


def _final(x, peer_t, g2, fg, t_len, tm):
    n = x.shape[0]
    g_a, g_s = _mod_operand(g2, t_len, tm)
    tile = pl.BlockSpec((tm, D_MODEL), lambda i: (i, 0))
    return pl.pallas_call(
        _final_kernel,
        grid=(n // tm,),
        in_specs=[tile, pl.BlockSpec((D_MODEL, tm), lambda i: (0, i)), g_s,
                  pl.BlockSpec((1, D_MODEL), lambda i: (0, 0))],
        out_specs=tile,
        out_shape=jax.ShapeDtypeStruct((n, D_MODEL), F32),
        compiler_params=_cparams("arbitrary"),
        name="final_norm",
    )(x, peer_t, g_a, fg)


def _transpose_kernel(x_ref, o_ref):
    o_ref[0] = x_ref[0].T.astype(BF)


def _transpose_bf16(x, tr=512):
    nl, r, c = x.shape
    return pl.pallas_call(
        _transpose_kernel,
        grid=(nl, r // tr),
        in_specs=[pl.BlockSpec((1, tr, c), lambda l, i: (l, i, 0))],
        out_specs=pl.BlockSpec((1, c, tr), lambda l, i: (l, 0, i)),
        out_shape=jax.ShapeDtypeStruct((nl, c, r), BF),
        compiler_params=_cparams("arbitrary", "arbitrary"),
        name="transpose_bf16",
    )(x)


def _t5_bucket(rel):
    n = jnp.maximum(rel, 0)
    max_exact = N_BUCKETS // 2
    nf = jnp.maximum(n, max_exact).astype(F32)
    large = max_exact + (jnp.log(nf / max_exact) / math.log(MAX_DISTANCE / max_exact)
                         * (N_BUCKETS - max_exact)).astype(jnp.int32)
    large = jnp.minimum(large, N_BUCKETS - 1)
    return jnp.where(n < max_exact, n, large)


def _bias_table(rel, rel_bias, causal):
    onehot = (_t5_bucket(rel)[..., None] == jnp.arange(N_BUCKETS, dtype=jnp.int32)).astype(F32)
    b = jnp.einsum("rcb,bh->hrc", onehot, rel_bias, precision=lax.Precision.HIGHEST)
    if causal:
        b = jnp.where((rel >= 0)[None], b, NEG)
    return b


def _prompt_bias(rel_bias):
    t = jnp.arange(MOBA_BLOCK, dtype=jnp.int32)
    d = t[:, None] - t[None, :]
    own = _bias_table(d, rel_bias, True)
    prev = _bias_table(d + MOBA_BLOCK, rel_bias, False)
    pair = lambda b: b.reshape(N_HEADS // 2, 2 * MOBA_BLOCK, MOBA_BLOCK)
    far = rel_bias[_t5_bucket(jnp.int32(MOBA_BLOCK + 1))]
    return pair(own), pair(prev), far


def _sample_bias(rel_bias, ds, past, nblk):
    t = jnp.arange(ds, dtype=jnp.int32)
    l = jnp.arange(MOBA_BLOCK, dtype=jnp.int32)
    blocks = []
    for j in range(nblk):
        rel = past + t[:, None] - (j * MOBA_BLOCK + l[None, :])
        blocks.append(_bias_table(rel, rel_bias, False).reshape(N_HEADS * ds, MOBA_BLOCK))
    own = _bias_table(t[:, None] - t[None, :], rel_bias, True).reshape(N_HEADS * ds, ds)
    own = jnp.concatenate([own, jnp.full((N_HEADS * ds, LANES - ds), NEG, F32)], axis=1)
    return jnp.stack(blocks), own


def _gate_weights(w_s, b_s, chunk_len):
    w = jnp.where(jnp.tril(jnp.ones((chunk_len, chunk_len), bool)), w_s[:, :chunk_len, :chunk_len], 0)
    reps = CHUNK // chunk_len
    if reps > 1:
        eye = jnp.eye(reps, dtype=w.dtype)
        w = jnp.einsum("ab,gts->gatbs", eye, w).reshape(A_GROUPS, CHUNK, CHUNK)
    bias = jnp.tile(b_s[:, :chunk_len].T, (reps, 1))
    bias = jnp.repeat(bias, D_MODEL // A_GROUPS, axis=1)
    return w.astype(BF), bias


def kernel(x_prompt, x_sample, cache_k, cache_v, page_table, c_prompt, c_sample, a_w_in, a_g_v, a_w_s, a_b_s, a_w_out, b_w_q, b_w_o, kv_g, w_kv, rel_bias, g_mix, g_ffn, w_ada, b_ada, p_w_q, p_sub_keys, p_w_u, p_w_v, final_g):
    batch, seq, _ = x_prompt.shape
    db, ds, _ = x_sample.shape
    n_p = batch * seq
    n_s = db * ds
    past = page_table.shape[1] * PAGE_SIZE
    row = lambda v: v.reshape(1, D_MODEL)

    w_in = a_w_in[0].astype(BF)
    w_out = a_w_out[0].astype(BF)
    wkv = w_kv.astype(BF)
    wq_attn = b_w_q[0].astype(BF)
    wo_attn = b_w_o[0].astype(BF)
    wqt = _transpose_bf16(p_w_q)
    sk = p_sub_keys.astype(BF)
    wu = p_w_u.astype(BF)
    wvt = _transpose_bf16(p_w_v)

    mod = _ada(jnp.concatenate([c_prompt, c_sample], axis=0), w_ada, b_ada)

    def mods(l, lo, hi):
        m = mod[l, lo:hi]
        return [m[:, k * D_MODEL:(k + 1) * D_MODEL] for k in range(6)]

    bias_own_p, bias_prev_p, c31 = _prompt_bias(rel_bias)
    bias_blk_s, bias_own_s = _sample_bias(rel_bias, ds, past, past // MOBA_BLOCK)

    def trunk(x, t_len, lo, hi, gate_w, gate_b, tm, attend):
        sh1, sc1, g1, sh2, sc2, g2 = mods(0, lo, hi)
        x1, v_rows = _gmlp(x, row(g_mix[0]), sc1, sh1, g1, w_in, row(a_g_v[0]), gate_w, gate_b, w_out,
                           t_len, tm)
        peer_t = _peer(x1, row(g_ffn[0]), sc2, sh2, wqt, sk, wu, wvt, 0, t_len)
        sh1b, sc1b, g1b, sh2b, sc2b, g2b = mods(1, lo, hi)
        x2, k, v, kb, vb, kmean, q = _kvq(x1, peer_t, g2, row(kv_g), wkv, row(g_mix[1]), sc1b, sh1b,
                                          wq_attn, t_len, t_len % MOBA_BLOCK == 0)
        att = attend(q, k, v, kb, vb, kmean)
        x3 = _proj_res(x2, att, g1b, wo_attn, t_len, MOBA_BLOCK)
        peer_t = _peer(x3, row(g_ffn[1]), sc2b, sh2b, wqt, sk, wu, wvt, 1, t_len)
        y = _final(x3, peer_t, g2b, row(final_g), t_len, MOBA_BLOCK)
        return y, k, v, v_rows

    def attend_prompt(q, k, v, kb, vb, kmean):
        return _attn_prompt(q, kb, vb, kmean, bias_own_p, bias_prev_p, c31, batch, seq)

    def attend_sample(q, k, v, kb, vb, kmean):
        out = _attn_sample(q.reshape(db, ds, D_MODEL), k.reshape(db, ds, D_MODEL), v.reshape(db, ds, D_MODEL),
                           cache_k, cache_v, page_table, bias_blk_s, bias_own_s)
        return out.reshape(n_s, D_MODEL)

    gw_p, gb_p = _gate_weights(a_w_s[0], a_b_s[0], min(seq, CHUNK))
    gw_s, gb_s = _gate_weights(a_w_s[0], a_b_s[0], min(ds, CHUNK))

    y_p, k_p, v_p, _ = trunk(x_prompt.reshape(n_p, D_MODEL), seq, 0, batch, gw_p, gb_p, 512, attend_prompt)
    y_s, k_s, v_s, cv_s = trunk(x_sample.reshape(n_s, D_MODEL), ds, batch, batch + db, gw_s, gb_s, 128,
                                attend_sample)

    heads = lambda a, b, t: a.reshape(b, t, N_HEADS, HEAD_DIM)
    heads_t = lambda a: a.reshape(batch, N_HEADS, HEAD_DIM, seq).transpose(0, 3, 1, 2)
    return (y_p.reshape(batch, seq, D_MODEL), y_s.reshape(db, ds, D_MODEL),
            heads_t(k_p), heads_t(v_p),
            heads(k_s, db, ds), heads(v_s, db, ds),
            cv_s.reshape(1, db, ds, D_MODEL))
```

```python
import functools
import math

import jax
import jax.numpy as jnp
from jax import lax
from jax.experimental import pallas as pl
from jax.experimental.pallas import tpu as pltpu

D_MODEL = 1024
N_HEADS = 16
HEAD_DIM = 64
CHUNK = 128
A_GROUPS = 8
MOBA_BLOCK = 256
MOBA_TOPK = 3
PAGE_SIZE = 128
N_BUCKETS = 32
MAX_DISTANCE = 128
PEER_HEADS = 8
PEER_NKEYS = 128
PEER_TOPK = 16
EPS = 1e-6
NEG = -1e30
NINF = float("-inf")
SCALE = HEAD_DIM ** -0.5

BF = jnp.bfloat16
F32 = jnp.float32

LANES = 128
VMEM_LIMIT = 56 * 1024 * 1024

_NT = (((1,), (1,)), ((), ()))


def _cparams(*sem):
    return pltpu.CompilerParams(dimension_semantics=sem, vmem_limit_bytes=VMEM_LIMIT)


def _rms(x, g):
    ms = jnp.mean(x * x, axis=-1, keepdims=True)
    return x * lax.rsqrt(ms + EPS) * g


def _dot(a, b):
    return jnp.dot(a, b, preferred_element_type=F32)


def _dot_nt(a, b):
    return lax.dot_general(a, b, _NT, preferred_element_type=F32)


def _ada_kernel(c_ref, w_ref, b_ref, o_ref):
    c = c_ref[...]
    s = (c * jax.nn.sigmoid(c)).astype(BF)
    o_ref[0] = _dot(s, w_ref[0].astype(BF)) + b_ref[0]


def _ada(c_all, w_ada, b_ada):
    m = c_all.shape[0]
    depth, _, n6 = w_ada.shape
    tn = 1536
    return pl.pallas_call(
        _ada_kernel,
        grid=(depth, n6 // tn),
        in_specs=[
            pl.BlockSpec((m, D_MODEL), lambda l, j: (0, 0)),
            pl.BlockSpec((1, D_MODEL, tn), lambda l, j: (l, 0, j)),
            pl.BlockSpec((1, 1, tn), lambda l, j: (l, 0, j)),
        ],
        out_specs=pl.BlockSpec((1, m, tn), lambda l, j: (l, 0, j)),
        out_shape=jax.ShapeDtypeStruct((depth, m, n6), F32),
        compiler_params=_cparams("arbitrary", "arbitrary"),
        name="ada",
    )(c_all, w_ada, b_ada.reshape(depth, 1, n6))


def _mod_operand(arr, t_len, tm):
    b = arr.shape[0]
    if t_len % tm == 0:
        per = t_len // tm
        return arr.reshape(b, 1, D_MODEL), pl.BlockSpec((1, 1, D_MODEL), lambda i, *_: (i // per, 0, 0))
    n = b * t_len
    rep = jnp.repeat(arr, t_len, axis=0).reshape(n // tm, tm, D_MODEL)
    return rep, pl.BlockSpec((1, tm, D_MODEL), lambda i, *_: (i, 0, 0))


def _gmlp_kernel(x_ref, gm_ref, sc_ref, sh_ref, g1_ref, win_ref, gv_ref, wg_ref, bs_ref, wout_ref,
                 xo_ref, v_ref, us_scr, *, tm):
    x = x_ref[...]
    h = _rms(x, gm_ref[...]) * (1.0 + sc_ref[0]) + sh_ref[0]
    uv = jax.nn.gelu(_dot(h.astype(BF), win_ref[...]))
    u = uv[:, :D_MODEL]
    v = _rms(uv[:, D_MODEL:], gv_ref[...])
    v_ref[...] = v
    vb = v.astype(BF)
    for c in range(tm // CHUNK):
        r0 = c * CHUNK
        parts = []
        for g in range(A_GROUPS):
            c0 = g * CHUNK
            parts.append(_dot(wg_ref[g], vb[r0:r0 + CHUNK, c0:c0 + CHUNK]))
        s = jnp.concatenate(parts, axis=1) + bs_ref[...]
        us_scr[r0:r0 + CHUNK, :] = (u[r0:r0 + CHUNK, :] * s).astype(BF)
    out = _dot(us_scr[...], wout_ref[...])
    xo_ref[...] = x + g1_ref[0] * out


def _gmlp(x, gm, sc, sh, g1, w_in, g_v, wg, bs, w_out, t_len, tm):
    n = x.shape[0]
    sc_a, sc_s = _mod_operand(sc, t_len, tm)
    sh_a, sh_s = _mod_operand(sh, t_len, tm)
    g1_a, g1_s = _mod_operand(g1, t_len, tm)
    full = lambda shape: pl.BlockSpec(shape, lambda i: (0,) * len(shape))
    return pl.pallas_call(
        functools.partial(_gmlp_kernel, tm=tm),
        grid=(n // tm,),
        in_specs=[
            pl.BlockSpec((tm, D_MODEL), lambda i: (i, 0)),
            full((1, D_MODEL)), sc_s, sh_s, g1_s,
            full((D_MODEL, 2 * D_MODEL)), full((1, D_MODEL)),
            full((A_GROUPS, CHUNK, CHUNK)), full((CHUNK, D_MODEL)),
            full((D_MODEL, D_MODEL)),
        ],
        out_specs=[pl.BlockSpec((tm, D_MODEL), lambda i: (i, 0)),
                   pl.BlockSpec((tm, D_MODEL), lambda i: (i, 0))],
        out_shape=[jax.ShapeDtypeStruct((n, D_MODEL), F32), jax.ShapeDtypeStruct((n, D_MODEL), F32)],
        scratch_shapes=[pltpu.VMEM((tm, D_MODEL), BF)],
        compiler_params=_cparams("arbitrary"),
        name="gmlp",
    )(x, gm, sc_a, sh_a, g1_a, w_in, g_v, wg, bs, w_out)


SUBLANES = 8


def _oddeven_sort_pairs(n):
    pairs = []
    p = 1
    while p < n:
        k = p
        while k >= 1:
            for j in range(k % p, n - k, 2 * k):
                for i in range(min(k, n - j - k)):
                    if (i + j) // (2 * p) == (i + j + k) // (2 * p):
                        pairs.append((i + j, i + j + k))
            k //= 2
        p *= 2
    return pairs


_SORT16 = _oddeven_sort_pairs(PEER_TOPK)
_BITONIC16 = [(i, i | k) for k in (8, 4, 2, 1) for i in range(PEER_TOPK) if not i & k]


def _cex(v, pairs):
    for i, j in pairs:
        v[i], v[j] = jnp.maximum(v[i], v[j]), jnp.minimum(v[i], v[j])


def _top16_sorted(v):
    v = list(v)
    _cex(v, _SORT16)
    for shift in (4, 2, 1):
        w = [pltpu.roll(x, shift, axis=0) for x in v]
        v = [jnp.maximum(v[k], w[PEER_TOPK - 1 - k]) for k in range(PEER_TOPK)]
        _cex(v, _BITONIC16)
    return v


def _kth_sum(a, b):
    shape = a[0].shape
    row = lax.broadcasted_iota(jnp.int32, shape, 0)

    def stack(xs):
        out = xs[SUBLANES - 1]
        for r in range(SUBLANES - 2, -1, -1):
            out = jnp.where(row == r, xs[r], out)
        return out

    a_lo, a_hi, b_hi = stack(a[:SUBLANES]), stack(a[SUBLANES:]), stack(b[SUBLANES:])
    cands = [a_lo + b[0], a_hi + b[0]]
    for l in range(1, SUBLANES):
        kmax = PEER_TOPK // (l + 1)
        c = a_lo + b[l]
        if kmax < SUBLANES:
            c = jnp.where(row < kmax, c, NINF)
        cands.append(c)
    cands.append(b_hi + a[0])
    cands += [jnp.full(shape, NINF, F32)] * (PEER_TOPK - len(cands))
    return _top16_sorted(cands)


def _route_kernel(x_ref, gm_ref, sc_ref, sh_ref, wqt_ref, sk_ref,
                  ht_ref, s0_ref, s1_ref, e0_ref, e1_ref, thr_ref,
                  qt_scr, *, tn):
    x = x_ref[...]
    h = _rms(x, gm_ref[...]) * (1.0 + sc_ref[0]) + sh_ref[0]
    ht = h.T.astype(BF)
    ht_ref[...] = ht
    qt_scr[...] = _dot(wqt_ref[...], ht).astype(BF)

    def head(hd, carry):
        st = []
        for p in range(2):
            r0 = pl.multiple_of((hd * 2 + p) * PEER_NKEYS, PEER_NKEYS)
            st.append(_dot(sk_ref[p, hd], qt_scr[pl.ds(r0, PEER_NKEYS), :]))
        for tb in range(tn // LANES):
            lanes = slice(tb * LANES, (tb + 1) * LANES)
            s0 = st[0][:, lanes]
            s1 = st[1][:, lanes]
            s0_ref[tb, hd] = s0
            s1_ref[tb, hd] = s1
            tiles = lambda s: [s[SUBLANES * r:SUBLANES * (r + 1)] for r in range(PEER_NKEYS // SUBLANES)]
            a = _top16_sorted(tiles(s0))
            b = _top16_sorted(tiles(s1))
            top = _kth_sum(a, b)
            z = functools.reduce(jnp.add, [jnp.exp(t - top[0]) for t in top[1:]]) + 1.0
            thr_ref[tb, hd] = top[PEER_TOPK - 1][0:1]
            e0_ref[tb, hd] = jnp.exp(s0 - a[0][0:1]) * (1.0 / z[0:1])
            e1_ref[tb, hd] = jnp.exp(s1 - b[0][0:1])
        return carry

    lax.fori_loop(0, PEER_HEADS, head, 0)


def _route(x, gm, sc, sh, wqt, sk, layer, t_len, tn):
    n = x.shape[0]
    sc_a, sc_s = _mod_operand(sc, t_len, tn)
    sh_a, sh_s = _mod_operand(sh, t_len, tn)
    full = lambda shape: pl.BlockSpec(shape, lambda i: (0,) * len(shape))
    nlb = tn // LANES
    sshape = jax.ShapeDtypeStruct((n // LANES, PEER_HEADS, PEER_NKEYS, LANES), F32)
    sspec = pl.BlockSpec((nlb, PEER_HEADS, PEER_NKEYS, LANES), lambda i: (i, 0, 0, 0))
    return pl.pallas_call(
        functools.partial(_route_kernel, tn=tn),
        grid=(n // tn,),
        in_specs=[
            pl.BlockSpec((tn, D_MODEL), lambda i: (i, 0)),
            full((1, D_MODEL)), sc_s, sh_s,
            pl.BlockSpec((None, 2 * PEER_HEADS * PEER_NKEYS, D_MODEL), lambda i: (layer, 0, 0)),
            pl.BlockSpec((None, 2, PEER_HEADS, PEER_NKEYS, PEER_NKEYS), lambda i: (layer, 0, 0, 0, 0)),
        ],
        out_specs=[pl.BlockSpec((D_MODEL, tn), lambda i: (0, i)), sspec, sspec, sspec, sspec,
                   pl.BlockSpec((nlb, PEER_HEADS, 1, LANES), lambda i: (i, 0, 0, 0))],
        out_shape=[jax.ShapeDtypeStruct((D_MODEL, n), BF), sshape, sshape, sshape, sshape,
                   jax.ShapeDtypeStruct((n // LANES, PEER_HEADS, 1, LANES), F32)],
        scratch_shapes=[pltpu.VMEM((2 * PEER_HEADS * PEER_NKEYS, tn), BF)],
        compiler_params=_cparams("arbitrary"),
        name="peer_route",
    )(x, gm, sc_a, sh_a, wqt, sk)


PEER_I_PER_STEP = 8


def _expert_kernel(ht_ref, s1_ref, e1_ref, s0_ref, e0_ref, thr_ref, wu_ref, wvt_ref, o_ref,
                   ga_scr, a_scr, *, tn):
    e = pl.program_id(1)
    a_scr[...] = _dot(wu_ref[...], ht_ref[...])
    jr = 16
    bcast = lambda r: jnp.broadcast_to(r, (jr, LANES))
    for tb in range(tn // LANES):
        lanes = slice(tb * LANES, (tb + 1) * LANES)

        def jgroup(jg, carry, tb=tb, lanes=lanes):
            j0 = pl.multiple_of(jg * jr, jr)
            g = [None] * PEER_I_PER_STEP
            for hd in range(PEER_HEADS):
                s1 = s1_ref[tb, hd, pl.ds(j0, jr), :]
                e1 = e1_ref[tb, hd, pl.ds(j0, jr), :]
                thr = bcast(thr_ref[tb, hd])
                for il in range(PEER_I_PER_STEP):
                    val = s1 + bcast(s0_ref[tb, hd, il:il + 1, :])
                    w = e1 * bcast(e0_ref[tb, hd, il:il + 1, :])
                    t = jnp.where(val >= thr, w, 0.0)
                    g[il] = t if g[il] is None else g[il] + t
            for il in range(PEER_I_PER_STEP):
                r0 = pl.multiple_of(il * PEER_NKEYS + j0, jr)
                ga_scr[pl.ds(r0, jr), lanes] = (jax.nn.gelu(a_scr[pl.ds(r0, jr), lanes]) * g[il]).astype(BF)
            return carry

        lax.fori_loop(0, PEER_NKEYS // jr, jgroup, 0)
    part = _dot(wvt_ref[...], ga_scr[...])

    @pl.when(e == 0)
    def _():
        o_ref[...] = part

    @pl.when(e > 0)
    def _():
        o_ref[...] += part


def _experts(ht, s0, s1, e0, e1, thr, wu, wvt, layer, tn):
    n = ht.shape[1]
    n_exp = wu.shape[1]
    te = PEER_I_PER_STEP * PEER_NKEYS
    nlb = tn // LANES
    return pl.pallas_call(
        functools.partial(_expert_kernel, tn=tn),
        grid=(n // tn, n_exp // te),
        in_specs=[
            pl.BlockSpec((D_MODEL, tn), lambda t, e: (0, t)),
            pl.BlockSpec((nlb, PEER_HEADS, PEER_NKEYS, LANES), lambda t, e: (t, 0, 0, 0)),
            pl.BlockSpec((nlb, PEER_HEADS, PEER_NKEYS, LANES), lambda t, e: (t, 0, 0, 0)),
            pl.BlockSpec((nlb, PEER_HEADS, PEER_I_PER_STEP, LANES), lambda t, e: (t, 0, e, 0)),
            pl.BlockSpec((nlb, PEER_HEADS, PEER_I_PER_STEP, LANES), lambda t, e: (t, 0, e, 0)),
            pl.BlockSpec((nlb, PEER_HEADS, 1, LANES), lambda t, e: (t, 0, 0, 0)),
            pl.BlockSpec((None, te, D_MODEL), lambda t, e: (layer, e, 0)),
            pl.BlockSpec((None, D_MODEL, te), lambda t, e: (layer, 0, e)),
        ],
        out_specs=pl.BlockSpec((D_MODEL, tn), lambda t, e: (0, t)),
        out_shape=jax.ShapeDtypeStruct((D_MODEL, n), F32),
        scratch_shapes=[pltpu.VMEM((te, tn), BF), pltpu.VMEM((te, tn), F32)],
        compiler_params=_cparams("arbitrary", "arbitrary"),
        name="peer_experts",
    )(ht, s1, e1, s0, e0, thr, wu, wvt)


def _peer(x, gm, sc, sh, wqt, sk, wu, wvt, layer, t_len):
    n = x.shape[0]
    ht, s0, s1, e0, e1, thr = _route(x, gm, sc, sh, wqt, sk, layer, t_len, 256)
    return _experts(ht, s0, s1, e0, e1, thr, wu, wvt, layer, min(1024, n))


def _kvq_kernel(x_ref, pt_ref, g2_ref, kvg_ref, wkv_ref, gm_ref, sc_ref, sh_ref, wq_ref,
                xo_ref, k_ref, v_ref, kb_ref, vb_ref, km_ref, q_ref, *, kv_transposed):
    x = x_ref[...] + g2_ref[0] * pt_ref[...].T
    xo_ref[...] = x
    kv = _dot(_rms(x, kvg_ref[...]).astype(BF), wkv_ref[...])
    k = kv[:, :D_MODEL]
    v = kv[:, D_MODEL:]
    if kv_transposed:
        k_ref[0] = k.T
        v_ref[0] = v.T
    else:
        k_ref[...] = k
        v_ref[...] = v
    kb_ref[...] = k.astype(BF)
    vb_ref[...] = v.astype(BF)
    km_ref[0] = jnp.mean(k, axis=0, keepdims=True)
    h = _rms(x, gm_ref[...]) * (1.0 + sc_ref[0]) + sh_ref[0]
    q_ref[...] = _dot(h.astype(BF), wq_ref[...])


def _kvq(x, peer_t, g2, kv_g, w_kv, gm, sc, sh, w_q, t_len, kv_transposed):
    n = x.shape[0]
    tm = MOBA_BLOCK
    g2_a, g2_s = _mod_operand(g2, t_len, tm)
    sc_a, sc_s = _mod_operand(sc, t_len, tm)
    sh_a, sh_s = _mod_operand(sh, t_len, tm)
    full = lambda shape: pl.BlockSpec(shape, lambda i: (0,) * len(shape))
    tile = pl.BlockSpec((tm, D_MODEL), lambda i: (i, 0))
    f32 = jax.ShapeDtypeStruct((n, D_MODEL), F32)
    bf = jax.ShapeDtypeStruct((n, D_MODEL), BF)
    if kv_transposed:
        per = t_len // tm
        kv_shape = jax.ShapeDtypeStruct((n // t_len, D_MODEL, t_len), F32)
        kv_spec = pl.BlockSpec((1, D_MODEL, tm), lambda i: (i // per, 0, i % per))
    else:
        kv_shape, kv_spec = f32, tile
    return pl.pallas_call(
        functools.partial(_kvq_kernel, kv_transposed=kv_transposed),
        grid=(n // tm,),
        in_specs=[tile, pl.BlockSpec((D_MODEL, tm), lambda i: (0, i)), g2_s,
                  full((1, D_MODEL)), full((D_MODEL, 2 * D_MODEL)),
                  full((1, D_MODEL)), sc_s, sh_s, full((D_MODEL, D_MODEL))],
        out_specs=[tile, kv_spec, kv_spec, tile, tile,
                   pl.BlockSpec((1, 1, D_MODEL), lambda i: (i, 0, 0)), tile],
        out_shape=[f32, kv_shape, kv_shape, bf, bf, jax.ShapeDtypeStruct((n // tm, 1, D_MODEL), F32), f32],
        compiler_params=_cparams("arbitrary"),
        name="kvq",
    )(x, peer_t, g2_a, kv_g, w_kv, gm, sc_a, sh_a, w_q)


def _third_largest(g):
    m1 = jnp.max(g, axis=1, keepdims=True)
    g2 = jnp.where(g == m1, NINF, g)
    m2 = jnp.max(g2, axis=1, keepdims=True)
    g3 = jnp.where(g2 == m2, NINF, g2)
    return jnp.max(g3, axis=1, keepdims=True)


def _attn_prompt_kernel(c31_ref, q_ref, k_ref, v_ref, km_ref, bo_ref, bp_ref, o_ref, *, nb):
    hp = pl.program_id(1)
    tq = MOBA_BLOCK
    lane = lax.broadcasted_iota(jnp.int32, (tq, LANES), 1)
    row = lax.broadcasted_iota(jnp.int32, (2 * tq, 1), 0)
    c31 = jnp.where(row < tq, c31_ref[2 * hp], c31_ref[2 * hp + 1])
    km = jnp.concatenate([km_ref[0], jnp.zeros((LANES - nb, LANES), F32)], axis=0)
    jidx = lax.broadcasted_iota(jnp.int32, (2 * tq, LANES), 1)

    for i in range(nb):
        q2 = q_ref[i * tq:(i + 1) * tq, :]
        qf = jnp.concatenate([jnp.where(lane < HEAD_DIM, q2, 0.0),
                              jnp.where(lane >= HEAD_DIM, q2, 0.0)], axis=0)
        qs = qf.astype(BF)

        def scores(j):
            return _dot_nt(qs, k_ref[j * tq:(j + 1) * tq, :]) * SCALE

        pieces = []
        if i > MOBA_TOPK:
            gate = lax.dot_general(qf, km, _NT, precision=lax.Precision.HIGHEST,
                                   preferred_element_type=F32)
            gm = jnp.where(jidx < i, gate, NINF)
            selpen = jnp.where(gm >= _third_largest(gm), 0.0, NEG)
            pen = lambda j: selpen[:, j:j + 1]
        else:
            pen = lambda j: 0.0
        for j in range(i - 1):
            pieces.append(scores(j) + (c31 + pen(j)))
        if i >= 1:
            pieces.append(scores(i - 1) + (bp_ref[0] + pen(i - 1)))
        pieces.append(scores(i) + bo_ref[0])

        m = functools.reduce(jnp.maximum, [jnp.max(s, axis=1, keepdims=True) for s in pieces])
        l = None
        acc = None
        for j, s in enumerate(pieces):
            p = jnp.exp(s - m)
            ls = jnp.sum(p, axis=1, keepdims=True)
            o = _dot(p.astype(BF), v_ref[j * tq:(j + 1) * tq, :])
            l = ls if l is None else l + ls
            acc = o if acc is None else acc + o
        out = acc / l
        o_ref[i * tq:(i + 1) * tq, :] = jnp.where(lane < HEAD_DIM, out[:tq], out[tq:]).astype(o_ref.dtype)


def _attn_prompt(q, kb, vb, kmean, bias_own, bias_prev, c31, batch, seq):
    n = q.shape[0]
    nb = seq // MOBA_BLOCK
    tq = MOBA_BLOCK
    grid_spec = pltpu.PrefetchScalarGridSpec(
        num_scalar_prefetch=1,
        grid=(batch, N_HEADS // 2),
        in_specs=[
            pl.BlockSpec((seq, LANES), lambda b, hp, c: (b, hp)),
            pl.BlockSpec((seq, LANES), lambda b, hp, c: (b, hp)),
            pl.BlockSpec((seq, LANES), lambda b, hp, c: (b, hp)),
            pl.BlockSpec((1, nb, LANES), lambda b, hp, c: (b, 0, hp)),
            pl.BlockSpec((1, 2 * tq, tq), lambda b, hp, c: (hp, 0, 0)),
            pl.BlockSpec((1, 2 * tq, tq), lambda b, hp, c: (hp, 0, 0)),
        ],
        out_specs=pl.BlockSpec((seq, LANES), lambda b, hp, c: (b, hp)),
    )
    return pl.pallas_call(
        functools.partial(_attn_prompt_kernel, nb=nb),
        grid_spec=grid_spec,
        out_shape=jax.ShapeDtypeStruct((n, D_MODEL), BF),
        compiler_params=_cparams("arbitrary", "arbitrary"),
        name="attn_prompt",
    )(c31, q, kb, vb, kmean.reshape(batch, nb, D_MODEL), bias_own, bias_prev)


SAMPLE_BLOCKS_PER_STEP = 8


def _attn_sample_kernel(pt_ref, q_ref, kn_ref, vn_ref, *refs, nblk, ds, ppb):
    bps = SAMPLE_BLOCKS_PER_STEP
    npg = bps * ppb
    k_refs = refs[:npg]
    v_refs = refs[npg:2 * npg]
    bias_ref, bown_ref, o_ref, qbd, o_scr, m_scr, l_scr, g_scr = refs[2 * npg:]
    step = pl.program_id(1)
    rows = N_HEADS * ds
    row = lax.broadcasted_iota(jnp.int32, (rows, D_MODEL), 0)
    col = lax.broadcasted_iota(jnp.int32, (rows, D_MODEL), 1)
    own = (col // HEAD_DIM) == (row // ds)

    @pl.when(step == 0)
    def _():
        qt = jnp.concatenate([q_ref[0]] * N_HEADS, axis=0)
        qbd[...] = jnp.where(own, qt, 0.0).astype(BF)

    qb = qbd[...]
    for bl in range(bps):
        blk = step * bps + bl
        kts = [k_refs[bl * ppb + pg][0] for pg in range(ppb)]
        vts = [v_refs[bl * ppb + pg][0] for pg in range(ppb)]
        s = jnp.concatenate([_dot(qb, kt.astype(BF)) for kt in kts], axis=1)
        gate = jnp.sum(s, axis=1, keepdims=True) * (1.0 / MOBA_BLOCK)
        s = s * SCALE + bias_ref[bl]
        m = jnp.max(s, axis=1, keepdims=True)
        p = jnp.exp(s - m)
        pb = p.astype(BF)
        o_scr[blk] = functools.reduce(jnp.add, [
            _dot_nt(pb[:, pg * PAGE_SIZE:(pg + 1) * PAGE_SIZE], vts[pg].astype(BF)) for pg in range(ppb)])
        m_scr[blk] = jnp.broadcast_to(m, (rows, LANES))
        l_scr[blk] = jnp.broadcast_to(jnp.sum(p, axis=1, keepdims=True), (rows, LANES))
        g_scr[blk] = jnp.broadcast_to(gate, (rows, LANES))

    @pl.when(step == pl.num_programs(1) - 1)
    def _():
        pad = jnp.zeros((LANES - ds, D_MODEL), F32)
        knb = jnp.concatenate([kn_ref[0], pad], axis=0).astype(BF)
        vnb = jnp.concatenate([vn_ref[0], pad], axis=0).astype(BF)
        so = _dot_nt(qb, knb) * SCALE + bown_ref[...]
        mo = jnp.max(so, axis=1, keepdims=True)
        po = jnp.exp(so - mo)
        lo = jnp.broadcast_to(jnp.sum(po, axis=1, keepdims=True), (rows, LANES))
        mo = jnp.broadcast_to(mo, (rows, LANES))
        oo = _dot(po.astype(BF), vnb)

        gs = [g_scr[b] for b in range(nblk)]
        g1 = functools.reduce(jnp.maximum, gs)
        gs2 = [jnp.where(g == g1, NINF, g) for g in gs]
        g2 = functools.reduce(jnp.maximum, gs2)
        gs3 = [jnp.where(g == g2, NINF, g) for g in gs2]
        g3 = functools.reduce(jnp.maximum, gs3)
        sel = [g >= g3 for g in gs]

        mm = mo
        for b in range(nblk):
            mm = jnp.maximum(mm, jnp.where(sel[b], m_scr[b], NINF))
        wide = lambda w: jnp.concatenate([w] * (D_MODEL // LANES), axis=1)
        wo = jnp.exp(mo - mm)
        lsum = lo * wo
        acc = oo * wide(wo)
        for b in range(nblk):
            wb = jnp.where(sel[b], jnp.exp(m_scr[b] - mm), 0.0)
            lsum = lsum + l_scr[b] * wb
            acc = acc + o_scr[b] * wide(wb)
        res = jnp.where(own, acc * wide(1.0 / lsum), 0.0)
        out = res[0:ds]
        for hh in range(1, N_HEADS):
            out = out + res[hh * ds:(hh + 1) * ds]
        o_ref[0] = out


def _attn_sample(q, kn, vn, cache_k, cache_v, page_table, bias_blk, bias_own):
    db, ds, _ = q.shape
    n_pages = page_table.shape[1]
    ppb = MOBA_BLOCK // PAGE_SIZE
    nblk = n_pages // ppb
    bps = SAMPLE_BLOCKS_PER_STEP
    npg = bps * ppb
    rows = N_HEADS * ds
    n_pool = cache_k.shape[0]
    ck = cache_k.transpose(0, 2, 3, 1).reshape(n_pool, D_MODEL, PAGE_SIZE)
    cv = cache_v.transpose(0, 2, 3, 1).reshape(n_pool, D_MODEL, PAGE_SIZE)
    pt = page_table.reshape(-1)

    def page_spec(k):
        return pl.BlockSpec((1, D_MODEL, PAGE_SIZE),
                            lambda s, j, pt_ref: (pt_ref[s * n_pages + j * npg + k], 0, 0))

    seq_spec = pl.BlockSpec((1, ds, D_MODEL), lambda s, j, pt_ref: (s, 0, 0))
    pages = [page_spec(k) for k in range(npg)]
    grid_spec = pltpu.PrefetchScalarGridSpec(
        num_scalar_prefetch=1,
        grid=(db, nblk // bps),
        in_specs=[seq_spec, seq_spec, seq_spec] + pages + pages + [
            pl.BlockSpec((bps, rows, MOBA_BLOCK), lambda s, j, pt_ref: (j, 0, 0)),
            pl.BlockSpec((rows, LANES), lambda s, j, pt_ref: (0, 0))],
        out_specs=seq_spec,
        scratch_shapes=[pltpu.VMEM((rows, D_MODEL), BF),
                        pltpu.VMEM((nblk, rows, D_MODEL), F32),
                        pltpu.VMEM((nblk, rows, LANES), F32),
                        pltpu.VMEM((nblk, rows, LANES), F32),
                        pltpu.VMEM((nblk, rows, LANES), F32)],
    )
    return pl.pallas_call(
        functools.partial(_attn_sample_kernel, nblk=nblk, ds=ds, ppb=ppb),
        grid_spec=grid_spec,
        out_shape=jax.ShapeDtypeStruct((db, ds, D_MODEL), F32),
        compiler_params=_cparams("arbitrary", "arbitrary"),
        name="attn_sample",
    )(pt, q, kn, vn, *([ck] * npg), *([cv] * npg), bias_blk, bias_own)


def _proj_res_kernel(x_ref, a_ref, g_ref, w_ref, o_ref):
    o_ref[...] = x_ref[...] + g_ref[0] * _dot(a_ref[...].astype(BF), w_ref[...])


def _proj_res(x, a, g, w, t_len, tm):
    n = x.shape[0]
    g_a, g_s = _mod_operand(g, t_len, tm)
    tile = pl.BlockSpec((tm, D_MODEL), lambda i: (i, 0))
    return pl.pallas_call(
        _proj_res_kernel,
        grid=(n // tm,),
        in_specs=[tile, tile, g_s, pl.BlockSpec((D_MODEL, D_MODEL), lambda i: (0, 0))],
        out_specs=tile,
        out_shape=jax.ShapeDtypeStruct((n, D_MODEL), F32),
        compiler_params=_cparams("arbitrary"),
        name="proj_res",
    )(x, a, g_a, w)


def _final_kernel(x_ref, pt_ref, g2_ref, fg_ref, o_ref):
    x = x_ref[...] + g2_ref[0] * pt_ref[...].T
    o_ref[...] = _rms(x, fg_ref[...])


def _final(x, peer_t, g2, fg, t_len, tm):
    n = x.shape[0]
    g_a, g_s = _mod_operand(g2, t_len, tm)
    tile = pl.BlockSpec((tm, D_MODEL), lambda i: (i, 0))
    return pl.pallas_call(
        _final_kernel,
        grid=(n // tm,),
        in_specs=[tile, pl.BlockSpec((D_MODEL, tm), lambda i: (0, i)), g_s,
                  pl.BlockSpec((1, D_MODEL), lambda i: (0, 0))],
        out_specs=tile,
        out_shape=jax.ShapeDtypeStruct((n, D_MODEL), F32),
        compiler_params=_cparams("arbitrary"),
        name="final_norm",
    )(x, peer_t, g_a, fg)


def _transpose_kernel(x_ref, o_ref):
    o_ref[0] = x_ref[0].T.astype(BF)


def _transpose_bf16(x, tr=512):
    nl, r, c = x.shape
    return pl.pallas_call(
        _transpose_kernel,
        grid=(nl, r // tr),
        in_specs=[pl.BlockSpec((1, tr, c), lambda l, i: (l, i, 0))],
        out_specs=pl.BlockSpec((1, c, tr), lambda l, i: (l, 0, i)),
        out_shape=jax.ShapeDtypeStruct((nl, c, r), BF),
        compiler_params=_cparams("arbitrary", "arbitrary"),
        name="transpose_bf16",
    )(x)


def _t5_bucket(rel):
    n = jnp.maximum(rel, 0)
    max_exact = N_BUCKETS // 2
    nf = jnp.maximum(n, max_exact).astype(F32)
    large = max_exact + (jnp.log(nf / max_exact) / math.log(MAX_DISTANCE / max_exact)
                         * (N_BUCKETS - max_exact)).astype(jnp.int32)
    large = jnp.minimum(large, N_BUCKETS - 1)
    return jnp.where(n < max_exact, n, large)


def _bias_table(rel, rel_bias, causal):
    onehot = (_t5_bucket(rel)[..., None] == jnp.arange(N_BUCKETS, dtype=jnp.int32)).astype(F32)
    b = jnp.einsum("rcb,bh->hrc", onehot, rel_bias, precision=lax.Precision.HIGHEST)
    if causal:
        b = jnp.where((rel >= 0)[None], b, NEG)
    return b


def _prompt_bias(rel_bias):
    t = jnp.arange(MOBA_BLOCK, dtype=jnp.int32)
    d = t[:, None] - t[None, :]
    own = _bias_table(d, rel_bias, True)
    prev = _bias_table(d + MOBA_BLOCK, rel_bias, False)
    pair = lambda b: b.reshape(N_HEADS // 2, 2 * MOBA_BLOCK, MOBA_BLOCK)
    far = rel_bias[_t5_bucket(jnp.int32(MOBA_BLOCK + 1))]
    return pair(own), pair(prev), far


def _sample_bias(rel_bias, ds, past, nblk):
    t = jnp.arange(ds, dtype=jnp.int32)
    l = jnp.arange(MOBA_BLOCK, dtype=jnp.int32)
    blocks = []
    for j in range(nblk):
        rel = past + t[:, None] - (j * MOBA_BLOCK + l[None, :])
        blocks.append(_bias_table(rel, rel_bias, False).reshape(N_HEADS * ds, MOBA_BLOCK))
    own = _bias_table(t[:, None] - t[None, :], rel_bias, True).reshape(N_HEADS * ds, ds)
    own = jnp.concatenate([own, jnp.full((N_HEADS * ds, LANES - ds), NEG, F32)], axis=1)
    return jnp.stack(blocks), own


def _gate_weights(w_s, b_s, chunk_len):
    w = jnp.where(jnp.tril(jnp.ones((chunk_len, chunk_len), bool)), w_s[:, :chunk_len, :chunk_len], 0)
    reps = CHUNK // chunk_len
    if reps > 1:
        eye = jnp.eye(reps, dtype=w.dtype)
        w = jnp.einsum("ab,gts->gatbs", eye, w).reshape(A_GROUPS, CHUNK, CHUNK)
    bias = jnp.tile(b_s[:, :chunk_len].T, (reps, 1))
    bias = jnp.repeat(bias, D_MODEL // A_GROUPS, axis=1)
    return w.astype(BF), bias


def kernel(x_prompt, x_sample, cache_k, cache_v, page_table, c_prompt, c_sample, a_w_in, a_g_v, a_w_s, a_b_s, a_w_out, b_w_q, b_w_o, kv_g, w_kv, rel_bias, g_mix, g_ffn, w_ada, b_ada, p_w_q, p_sub_keys, p_w_u, p_w_v, final_g):
    batch, seq, _ = x_prompt.shape
    db, ds, _ = x_sample.shape
    n_p = batch * seq
    n_s = db * ds
    past = page_table.shape[1] * PAGE_SIZE
    row = lambda v: v.reshape(1, D_MODEL)

    w_in = a_w_in[0].astype(BF)
    w_out = a_w_out[0].astype(BF)
    wkv = w_kv.astype(BF)
    wq_attn = b_w_q[0].astype(BF)
    wo_attn = b_w_o[0].astype(BF)
    wqt = _transpose_bf16(p_w_q)
    sk = p_sub_keys.astype(BF)
    wu = p_w_u.astype(BF)
    wvt = _transpose_bf16(p_w_v)

    mod = _ada(jnp.concatenate([c_prompt, c_sample], axis=0), w_ada, b_ada)

    def mods(l, lo, hi):
        m = mod[l, lo:hi]
        return [m[:, k * D_MODEL:(k + 1) * D_MODEL] for k in range(6)]

    bias_own_p, bias_prev_p, c31 = _prompt_bias(rel_bias)
    bias_blk_s, bias_own_s = _sample_bias(rel_bias, ds, past, past // MOBA_BLOCK)

    def trunk(x, t_len, lo, hi, gate_w, gate_b, tm, attend):
        sh1, sc1, g1, sh2, sc2, g2 = mods(0, lo, hi)
        x1, v_rows = _gmlp(x, row(g_mix[0]), sc1, sh1, g1, w_in, row(a_g_v[0]), gate_w, gate_b, w_out,
                           t_len, tm)
        peer_t = _peer(x1, row(g_ffn[0]), sc2, sh2, wqt, sk, wu, wvt, 0, t_len)
        sh1b, sc1b, g1b, sh2b, sc2b, g2b = mods(1, lo, hi)
        x2, k, v, kb, vb, kmean, q = _kvq(x1, peer_t, g2, row(kv_g), wkv, row(g_mix[1]), sc1b, sh1b,
                                          wq_attn, t_len, t_len % MOBA_BLOCK == 0)
        att = attend(q, k, v, kb, vb, kmean)
        x3 = _proj_res(x2, att, g1b, wo_attn, t_len, MOBA_BLOCK)
        peer_t = _peer(x3, row(g_ffn[1]), sc2b, sh2b, wqt, sk, wu, wvt, 1, t_len)
        y = _final(x3, peer_t, g2b, row(final_g), t_len, MOBA_BLOCK)
        return y, k, v, v_rows

    def attend_prompt(q, k, v, kb, vb, kmean):
        return _attn_prompt(q, kb, vb, kmean, bias_own_p, bias_prev_p, c31, batch, seq)

    def attend_sample(q, k, v, kb, vb, kmean):
        out = _attn_sample(q.reshape(db, ds, D_MODEL), k.reshape(db, ds, D_MODEL), v.reshape(db, ds, D_MODEL),
                           cache_k, cache_v, page_table, bias_blk_s, bias_own_s)
        return out.reshape(n_s, D_MODEL)

    gw_p, gb_p = _gate_weights(a_w_s[0], a_b_s[0], min(seq, CHUNK))
    gw_s, gb_s = _gate_weights(a_w_s[0], a_b_s[0], min(ds, CHUNK))

    y_p, k_p, v_p, _ = trunk(x_prompt.reshape(n_p, D_MODEL), seq, 0, batch, gw_p, gb_p, 512, attend_prompt)
    y_s, k_s, v_s, cv_s = trunk(x_sample.reshape(n_s, D_MODEL), ds, batch, batch + db, gw_s, gb_s, 128,
                                attend_sample)

    heads = lambda a, b, t: a.reshape(b, t, N_HEADS, HEAD_DIM)
    heads_t = lambda a: a.reshape(batch, N_HEADS, HEAD_DIM, seq).transpose(0, 3, 1, 2)
    return (y_p.reshape(batch, seq, D_MODEL), y_s.reshape(db, ds, D_MODEL),
            heads_t(k_p), heads_t(v_p),
            heads(k_s, db, ds), heads(v_s, db, ds),
            cv_s.reshape(1, db, ds, D_MODEL))
```
